```python
import math
import jax, jax.numpy as jnp
from jax import lax
import numpy as np

D_MODEL = 2048
BATCH = 4
SEQ = 2048
DEPTH = 2
DEC_BATCH = 32
DEC_SEQ = 1
PAST_LEN = 8192
PAGE_SIZE = 128

HEAD_DIM = 128
ROPE_THETA = 10000.0
NORM_EPS = 1e-6
D_FF = 5632
DIFF_HEADS = D_MODEL // (2 * HEAD_DIM)
DIFF_KV_HEADS = 4
DIFF_REP = DIFF_HEADS // DIFF_KV_HEADS
DIFF_QBLOCK = 128
NSA_HEADS = D_MODEL // HEAD_DIM
NSA_GROUPS = 4
NSA_REP = NSA_HEADS // NSA_GROUPS
CMP_LEN = 32
CMP_STRIDE = 16
CMP_HIDDEN = 256
SEL_LEN = 64
SEL_TOPK = 16
WINDOW = 512
NSA_QBLOCK = 32
SEL_FORCE = 1e30
N_DIFF_LAYERS = (DEPTH + 1) // 2
N_NSA_LAYERS = DEPTH // 2

kernel_name = 'hybrid_diffattn_nsa_macaron_decode_step'


def rmsnorm(x, g):
    x32 = x.astype(jnp.float32)
    y = x32 * lax.rsqrt(jnp.mean(x32 * x32, axis=-1, keepdims=True) + NORM_EPS)
    return (y * g.astype(jnp.float32)).astype(x.dtype)


def swiglu(x, w_in, w_out):
    gate, up = jnp.split(x @ w_in, 2, axis=-1)
    return (jax.nn.silu(gate) * up) @ w_out


def rope(x, pos):
    half = HEAD_DIM // 2
    inv = ROPE_THETA ** (-jnp.arange(half, dtype=jnp.float32) / half)
    ang = pos.astype(jnp.float32)[:, None] * inv[None, :]
    cos, sin = jnp.cos(ang)[:, None, :], jnp.sin(ang)[:, None, :]
    x32 = x.astype(jnp.float32)
    x1, x2 = x32[..., :half], x32[..., half:]
    return jnp.concatenate([x1 * cos - x2 * sin, x2 * cos + x1 * sin], axis=-1).astype(x.dtype)


def masked_softmax(s, mask):
    s = jnp.where(mask, s.astype(jnp.float32), -jnp.inf)
    m = jnp.max(s, axis=-1, keepdims=True)
    m = jnp.where(jnp.isfinite(m), m, 0.0)
    e = jnp.exp(s - m)
    return e / jnp.maximum(jnp.sum(e, axis=-1, keepdims=True), 1e-30)


def diff_lambda_value(lam, lam_init):
    lam = lam.astype(jnp.float32)
    return jnp.exp(jnp.sum(lam[0] * lam[1])) - jnp.exp(jnp.sum(lam[2] * lam[3])) + lam_init


def diff_project(h, w_qkv, pos):
    B, T, _ = h.shape
    qd = 2 * DIFF_HEADS * HEAD_DIM
    kd = 2 * DIFF_KV_HEADS * HEAD_DIM
    qkv = h @ w_qkv
    q = rope(qkv[..., :qd].reshape(B, T, 2 * DIFF_HEADS, HEAD_DIM), pos)
    k = rope(qkv[..., qd:qd + kd].reshape(B, T, 2 * DIFF_KV_HEADS, HEAD_DIM), pos)
    v = qkv[..., qd + kd:].reshape(B, T, DIFF_KV_HEADS, 2 * HEAD_DIM)
    q = q.reshape(B, T, DIFF_KV_HEADS, DIFF_REP, 2 * HEAD_DIM) * HEAD_DIM ** -0.5
    k = k.reshape(B, T, DIFF_KV_HEADS, 2 * HEAD_DIM)
    return q, k, v


def diff_output(o, subln, lam_init, w_o, dtype):
    B, T = o.shape[:2]
    o = rmsnorm(o.reshape(B, T, DIFF_HEADS, 2 * HEAD_DIM), subln) * (1.0 - lam_init)
    return o.reshape(B, T, DIFF_HEADS * 2 * HEAD_DIM).astype(dtype) @ w_o


def diff_prompt(h, w_qkv, lam, subln, w_o, lam_init):
    B, T, _ = h.shape
    pos = jnp.arange(T)
    q, k, v = diff_project(h, w_qkv, pos)
    lmb = diff_lambda_value(lam, lam_init)
    k1, k2 = k[..., :HEAD_DIM], k[..., HEAD_DIM:]

    def q_block(i):
        s0 = i * DIFF_QBLOCK
        qb = lax.dynamic_slice_in_dim(q, s0, DIFF_QBLOCK, axis=1)
        mask = pos[None, :] <= (s0 + jnp.arange(DIFF_QBLOCK))[:, None]
        p1 = masked_softmax(jnp.einsum('bqgrd,bkgd->bgrqk', qb[..., :HEAD_DIM], k1), mask)
        p2 = masked_softmax(jnp.einsum('bqgrd,bkgd->bgrqk', qb[..., HEAD_DIM:], k2), mask)
        a = (p1 - lmb * p2).astype(v.dtype)
        return jnp.einsum('bgrqk,bkgd->bqgrd', a, v, preferred_element_type=jnp.float32)

    o = lax.map(q_block, jnp.arange(T // DIFF_QBLOCK))
    o = jnp.moveaxis(o, 0, 1).reshape(B, T, DIFF_KV_HEADS, DIFF_REP, 2 * HEAD_DIM)
    return diff_output(o, subln, lam_init, w_o, h.dtype), jnp.stack([k, v], axis=2)


def diff_sample(h, cache, layer, page_table, w_qkv, lam, subln, w_o, lam_init):
    B, T, _ = h.shape
    pos = PAST_LEN + jnp.arange(T)
    q, k, v = diff_project(h, w_qkv, pos)
    lmb = diff_lambda_value(lam, lam_init)
    q1, q2 = q[..., :HEAD_DIM], q[..., HEAD_DIM:]

    def update(st, s, vv, mask):
        m, den, acc = st
        s = jnp.where(mask, s.astype(jnp.float32), -jnp.inf)
        m_new = jnp.maximum(m, jnp.max(s, axis=-1))
        m_ref = jnp.where(jnp.isfinite(m_new), m_new, 0.0)
        corr = jnp.exp(m - m_ref)
        p = jnp.exp(s - m_ref[..., None])
        den = corr * den + jnp.sum(p, axis=-1)
        acc = corr[..., None] * acc + jnp.einsum('bgrqk,bkgd->bgrqd', p.astype(vv.dtype), vv,
                                                 preferred_element_type=jnp.float32)
        return (m_new, den, acc)

    def both_maps(carry, kk, vv, mask):
        st1, st2 = carry
        st1 = update(st1, jnp.einsum('bqgrd,bkgd->bgrqk', q1, kk[..., :HEAD_DIM]), vv, mask)
        st2 = update(st2, jnp.einsum('bqgrd,bkgd->bgrqk', q2, kk[..., HEAD_DIM:]), vv, mask)
        return (st1, st2)

    def page_step(carry, phys):
        rows = cache[layer, phys]
        return both_maps(carry, rows[:, :, 0], rows[:, :, 1], True), None

    init = (jnp.full((B, DIFF_KV_HEADS, DIFF_REP, T), -jnp.inf, jnp.float32),
            jnp.zeros((B, DIFF_KV_HEADS, DIFF_REP, T), jnp.float32),
            jnp.zeros((B, DIFF_KV_HEADS, DIFF_REP, T, 2 * HEAD_DIM), jnp.float32))
    carry, _ = lax.scan(page_step, (init, init), page_table.T)
    carry = both_maps(carry, k, v, pos[None, :] <= pos[:, None])
    (_, d1, a1), (_, d2, a2) = carry
    o = a1 / d1[..., None] - lmb * (a2 / d2[..., None])
    o = jnp.transpose(o, (0, 3, 1, 2, 4))
    return diff_output(o, subln, lam_init, w_o, h.dtype), jnp.stack([k, v], axis=2)


def nsa_project(h, w_in, pos):
    B, T, _ = h.shape
    qd = NSA_HEADS * HEAD_DIM
    kd = NSA_GROUPS * HEAD_DIM
    proj = h @ w_in
    q = rope(proj[..., :qd].reshape(B, T, NSA_HEADS, HEAD_DIM), pos)
    kv = proj[..., qd:qd + 6 * kd].reshape(B, T, 3, 2, NSA_GROUPS, HEAD_DIM)
    k = rope(kv[:, :, :, 0].reshape(B, T, 3 * NSA_GROUPS, HEAD_DIM), pos).reshape(B, T, 3, NSA_GROUPS, HEAD_DIM)
    kv = jnp.stack([k, kv[:, :, :, 1]], axis=3)
    gates = jax.nn.sigmoid(proj[..., qd + 6 * kd:].astype(jnp.float32)).reshape(B, T, NSA_HEADS, 3)
    return q, kv[:, :, 0], kv[:, :, 1], kv[:, :, 2], gates


def compress(rows, pe, w1, b1, w2):
    B, L = rows.shape[:2]
    n_chunk = L // CMP_STRIDE
    chunks = rows[:, :n_chunk * CMP_STRIDE].reshape(B, n_chunk, CMP_STRIDE, 2, NSA_GROUPS, HEAD_DIM)
    first = jnp.einsum('bcrngd,nrdh->bcngh', chunks, w1[:, :CMP_STRIDE])
    second = jnp.einsum('bcrngd,nrdh->bcngh', chunks, w1[:, CMP_STRIDE:])
    bias = jnp.einsum('nrd,nrdh->nh', pe, w1) + b1
    hid = jax.nn.silu(first[:, :-1] + second[:, 1:] + bias[:, None, :])
    return jnp.einsum('bjngh,nhd->bjngd', hid, w2)


def sel_cover(n_cmp, n_sel):
    start = jnp.arange(n_cmp)[:, None] * CMP_STRIDE
    bstart = jnp.arange(n_sel)[None, :] * SEL_LEN
    return ((start < bstart + SEL_LEN) & (start + CMP_LEN > bstart)).astype(jnp.float32)


def nsa_attend(q, pos_q, kvc, kvw, pos_w, gather_sel, n_sel, gates):
    B, Tq = q.shape[:2]
    dt = kvc.dtype
    qg = q.reshape(B, Tq, NSA_GROUPS, NSA_REP, HEAD_DIM) * HEAD_DIM ** -0.5
    n_cmp = kvc.shape[1]
    cmp_end = jnp.arange(n_cmp) * CMP_STRIDE + (CMP_LEN - 1)
    mask_c = (cmp_end[None, :] <= pos_q[:, None])[None, :, None, None, :]
    p_c = masked_softmax(jnp.einsum('btgrd,bjgd->btgrj', qg, kvc[:, :, 0]), mask_c)
    o_c = jnp.einsum('btgrj,bjgd->btgrd', p_c.astype(dt), kvc[:, :, 1], preferred_element_type=jnp.float32)
    importance = jnp.einsum('btgrj,js->btgs', p_c, sel_cover(n_cmp, n_sel))
    blk = jnp.arange(n_sel)[None, :]
    cur = (pos_q // SEL_LEN)[:, None]
    forced = (blk == 0) | (blk == cur) | (blk == cur - 1)
    score = jnp.where(forced[None, :, None, :], SEL_FORCE,
                      jnp.where((blk <= cur)[None, :, None, :], importance, -SEL_FORCE))
    _, idx = lax.top_k(score, min(SEL_TOPK, n_sel))
    n_k = idx.shape[-1]
    kvs = gather_sel(idx)
    key_pos = idx[..., None] * SEL_LEN + jnp.arange(SEL_LEN)
    mask_s = (key_pos <= pos_q[None, :, None, None, None]).reshape(B, Tq, NSA_GROUPS, 1, n_k * SEL_LEN)
    ks = kvs[..., 0, :].reshape(B, Tq, NSA_GROUPS, n_k * SEL_LEN, HEAD_DIM)
    vs = kvs[..., 1, :].reshape(B, Tq, NSA_GROUPS, n_k * SEL_LEN, HEAD_DIM)
    p_s = masked_softmax(jnp.einsum('btgrd,btgnd->btgrn', qg, ks), mask_s)
    o_s = jnp.einsum('btgrn,btgnd->btgrd', p_s.astype(dt), vs, preferred_element_type=jnp.float32)
    rel = pos_q[:, None] - pos_w[None, :]
    mask_w = ((rel >= 0) & (rel < WINDOW) & (pos_w[None, :] >= 0))[None, :, None, None, :]
    p_w = masked_softmax(jnp.einsum('btgrd,bpgd->btgrp', qg, kvw[:, :, 0]), mask_w)
    o_w = jnp.einsum('btgrp,bpgd->btgrd', p_w.astype(dt), kvw[:, :, 1], preferred_element_type=jnp.float32)
    g = gates.reshape(B, Tq, NSA_GROUPS, NSA_REP, 3)
    o = g[..., 0:1] * o_c + g[..., 1:2] * o_s + g[..., 2:3] * o_w
    return o.reshape(B, Tq, NSA_HEADS * HEAD_DIM)


def nsa_prompt(h, w_in, pe, w1, b1, w2, w_o):
    B, T, _ = h.shape
    pos = jnp.arange(T)
    q, kvc_rows, kvs_rows, kvw_rows, gates = nsa_project(h, w_in, pos)
    kvc = compress(kvc_rows, pe, w1, b1, w2)
    n_sel = -(-T // SEL_LEN)
    sel_blocks = kvs_rows.reshape(B, n_sel, SEL_LEN, 2, NSA_GROUPS, HEAD_DIM)
    b_idx = jnp.arange(B)[:, None, None, None]
    g_idx = jnp.arange(NSA_GROUPS)[None, None, :, None]

    def gather_sel(idx):
        return sel_blocks[b_idx, idx, :, :, g_idx, :]

    kvw_pad = jnp.pad(kvw_rows, ((0, 0), (WINDOW, 0), (0, 0), (0, 0), (0, 0)))

    def q_block(i):
        s0 = i * NSA_QBLOCK
        qb = lax.dynamic_slice_in_dim(q, s0, NSA_QBLOCK, axis=1)
        gb = lax.dynamic_slice_in_dim(gates, s0, NSA_QBLOCK, axis=1)
        kvw = lax.dynamic_slice_in_dim(kvw_pad, s0, WINDOW + NSA_QBLOCK, axis=1)
        pos_w = s0 - WINDOW + jnp.arange(WINDOW + NSA_QBLOCK)
        return nsa_attend(qb, s0 + jnp.arange(NSA_QBLOCK), kvc, kvw, pos_w, gather_sel, n_sel, gb)

    o = lax.map(q_block, jnp.arange(T // NSA_QBLOCK))
    o = jnp.moveaxis(o, 0, 1).reshape(B, T, NSA_HEADS * HEAD_DIM)
    win = kvw_rows[:, T - min(WINDOW, T):]
    return o.astype(h.dtype) @ w_o, kvc_rows, kvs_rows, win


def nsa_sample(h, cache_cmp, cache_sel, win_state, layer, page_table, w_in, pe, w1, b1, w2, w_o):
    B, T, _ = h.shape
    pos = PAST_LEN + jnp.arange(T)
    q, kvc_new, kvs_new, kvw_new, gates = nsa_project(h, w_in, pos)
    past_c = cache_cmp[layer, page_table].reshape(B, PAST_LEN, 2, NSA_GROUPS, HEAD_DIM)
    kvc = compress(jnp.concatenate([past_c, kvc_new], axis=1), pe, w1, b1, w2)
    n_past_blk = PAST_LEN // SEL_LEN
    n_new_blk = -(-T // SEL_LEN)
    n_sel = n_past_blk + n_new_blk
    sub = PAGE_SIZE // SEL_LEN
    new_blocks = jnp.pad(kvs_new, ((0, 0), (0, n_new_blk * SEL_LEN - T), (0, 0), (0, 0), (0, 0)))
    new_blocks = new_blocks.reshape(B, n_new_blk, SEL_LEN, 2, NSA_GROUPS, HEAD_DIM)
    b_idx = jnp.arange(B)[:, None, None, None]
    g_idx = jnp.arange(NSA_GROUPS)[None, None, :, None]

    def gather_sel(idx):
        in_past = idx < n_past_blk
        ip = jnp.minimum(idx, n_past_blk - 1)
        phys = page_table[b_idx, ip // sub]
        row = (ip % sub)[..., None] * SEL_LEN + jnp.arange(SEL_LEN)
        past = cache_sel[layer, phys[..., None], row, :, g_idx[..., None], :]
        new = new_blocks[b_idx, jnp.clip(idx - n_past_blk, 0, n_new_blk - 1), :, :, g_idx, :]
        return jnp.where(in_past[..., None, None, None], past, new)

    wb = win_state.shape[2]
    kvw = jnp.concatenate([win_state[layer], kvw_new], axis=1)
    pos_w = PAST_LEN - wb + jnp.arange(wb + T)
    o = nsa_attend(q, pos, kvc, kvw, pos_w, gather_sel, n_sel, gates)
    return o.astype(h.dtype) @ w_o, kvc_new, kvs_new, kvw[:, T:]


def setup_inputs(seed: int = 0) -> dict:
    key = jax.random.key(seed)
    ks = jax.random.split(key, 26)

    def nrm(k, shape, scale=1.0):
        return jax.random.normal(k, shape, jnp.float32) * scale

    n_pages = PAST_LEN // PAGE_SIZE
    n_pool = (5 * DEC_BATCH * n_pages) // 4
    win_buf = min(WINDOW, PAST_LEN)
    page_table = jax.random.permutation(ks[6], n_pool)[:DEC_BATCH * n_pages]
    page_table = page_table.reshape(DEC_BATCH, n_pages).astype(jnp.int32)
    diff_in = 2 * DIFF_HEADS * HEAD_DIM + 4 * DIFF_KV_HEADS * HEAD_DIM
    nsa_in = NSA_HEADS * HEAD_DIM + 6 * NSA_GROUPS * HEAD_DIM + 3 * NSA_HEADS
    return {
        'x_prompt': nrm(ks[0], (BATCH, SEQ, D_MODEL)),
        'x_sample': nrm(ks[1], (DEC_BATCH, DEC_SEQ, D_MODEL)),
        'cache_diff': nrm(ks[2], (N_DIFF_LAYERS, n_pool, PAGE_SIZE, 2, DIFF_KV_HEADS, 2 * HEAD_DIM)),
        'cache_nsa_cmp': nrm(ks[3], (N_NSA_LAYERS, n_pool, PAGE_SIZE, 2, NSA_GROUPS, HEAD_DIM)),
        'cache_nsa_sel': nrm(ks[4], (N_NSA_LAYERS, n_pool, PAGE_SIZE, 2, NSA_GROUPS, HEAD_DIM)),
        'state_nsa_win': nrm(ks[5], (N_NSA_LAYERS, DEC_BATCH, win_buf, 2, NSA_GROUPS, HEAD_DIM)),
        'page_table': page_table,
        'norm_ffn1': 1.0 + nrm(ks[7], (DEPTH, D_MODEL), 0.01),
        'w_ffn1_in': nrm(ks[8], (DEPTH, D_MODEL, 2 * D_FF), D_MODEL ** -0.5),
        'w_ffn1_out': nrm(ks[9], (DEPTH, D_FF, D_MODEL), D_FF ** -0.5),
        'norm_mix': 1.0 + nrm(ks[10], (DEPTH, D_MODEL), 0.01),
        'norm_ffn2': 1.0 + nrm(ks[11], (DEPTH, D_MODEL), 0.01),
        'w_ffn2_in': nrm(ks[12], (DEPTH, D_MODEL, 2 * D_FF), D_MODEL ** -0.5),
        'w_ffn2_out': nrm(ks[13], (DEPTH, D_FF, D_MODEL), D_FF ** -0.5),
        'w_diff_qkv': nrm(ks[14], (N_DIFF_LAYERS, D_MODEL, diff_in), D_MODEL ** -0.5),
        'diff_lambda': nrm(ks[15], (N_DIFF_LAYERS, 4, HEAD_DIM), 0.1),
        'diff_subln': 1.0 + nrm(ks[16], (N_DIFF_LAYERS, 2 * HEAD_DIM), 0.01),
        'w_diff_o': nrm(ks[17], (N_DIFF_LAYERS, 2 * DIFF_HEADS * HEAD_DIM, D_MODEL), (2 * DIFF_HEADS * HEAD_DIM) ** -0.5),
        'w_nsa_in': nrm(ks[18], (N_NSA_LAYERS, D_MODEL, nsa_in), D_MODEL ** -0.5),
        'nsa_cmp_pe': nrm(ks[19], (N_NSA_LAYERS, 2, CMP_LEN, HEAD_DIM), 0.02),
        'nsa_cmp_w1': nrm(ks[20], (N_NSA_LAYERS, 2, CMP_LEN, HEAD_DIM, CMP_HIDDEN), (CMP_LEN * HEAD_DIM) ** -0.5),
        'nsa_cmp_b1': nrm(ks[21], (N_NSA_LAYERS, 2, CMP_HIDDEN), 0.01),
        'nsa_cmp_w2': nrm(ks[22], (N_NSA_LAYERS, 2, CMP_HIDDEN, HEAD_DIM), CMP_HIDDEN ** -0.5),
        'w_nsa_o': nrm(ks[23], (N_NSA_LAYERS, NSA_HEADS * HEAD_DIM, D_MODEL), (NSA_HEADS * HEAD_DIM) ** -0.5),
        'norm_final': 1.0 + nrm(ks[24], (D_MODEL,), 0.01),
    }


def reference(x_prompt, x_sample, cache_diff, cache_nsa_cmp, cache_nsa_sel, state_nsa_win, page_table,
              norm_ffn1, w_ffn1_in, w_ffn1_out, norm_mix, norm_ffn2, w_ffn2_in, w_ffn2_out,
              w_diff_qkv, diff_lambda, diff_subln, w_diff_o,
              w_nsa_in, nsa_cmp_pe, nsa_cmp_w1, nsa_cmp_b1, nsa_cmp_w2, w_nsa_o, norm_final):
    xp, xs = x_prompt, x_sample
    diff_p, diff_s, cmp_p, cmp_s, sel_p, sel_s, win_p, win_s = [], [], [], [], [], [], [], []
    for i in range(DEPTH):
        xp = xp + 0.5 * swiglu(rmsnorm(xp, norm_ffn1[i]), w_ffn1_in[i], w_ffn1_out[i])
        xs = xs + 0.5 * swiglu(rmsnorm(xs, norm_ffn1[i]), w_ffn1_in[i], w_ffn1_out[i])
        hp, hs = rmsnorm(xp, norm_mix[i]), rmsnorm(xs, norm_mix[i])
        layer = i // 2
        if i % 2 == 0:
            lam_init = 0.8 - 0.6 * math.exp(-0.3 * i)
            yp, kv_p = diff_prompt(hp, w_diff_qkv[layer], diff_lambda[layer], diff_subln[layer],
                                   w_diff_o[layer], lam_init)
            ys, kv_s = diff_sample(hs, cache_diff, layer, page_table, w_diff_qkv[layer], diff_lambda[layer],
                                   diff_subln[layer], w_diff_o[layer], lam_init)
            diff_p.append(kv_p)
            diff_s.append(kv_s)
        else:
            cw = (nsa_cmp_pe[layer], nsa_cmp_w1[layer], nsa_cmp_b1[layer], nsa_cmp_w2[layer])
            yp, c_p, s_p, w_p = nsa_prompt(hp, w_nsa_in[layer], *cw, w_nsa_o[layer])
            ys, c_s, s_s, w_s = nsa_sample(hs, cache_nsa_cmp, cache_nsa_sel, state_nsa_win, layer, page_table,
                                           w_nsa_in[layer], *cw, w_nsa_o[layer])
            cmp_p.append(c_p)
            cmp_s.append(c_s)
            sel_p.append(s_p)
            sel_s.append(s_s)
            win_p.append(w_p)
            win_s.append(w_s)
        xp = xp + yp
        xs = xs + ys
        xp = xp + 0.5 * swiglu(rmsnorm(xp, norm_ffn2[i]), w_ffn2_in[i], w_ffn2_out[i])
        xs = xs + 0.5 * swiglu(rmsnorm(xs, norm_ffn2[i]), w_ffn2_in[i], w_ffn2_out[i])
    y_prompt = rmsnorm(xp, norm_final)
    y_sample = rmsnorm(xs, norm_final)
    new_diff_prompt = jnp.stack(diff_p, axis=0)
    new_diff_sample = jnp.stack(diff_s, axis=0)
    new_nsa_cmp_prompt = jnp.stack(cmp_p, axis=0)
    new_nsa_cmp_sample = jnp.stack(cmp_s, axis=0)
    new_nsa_sel_prompt = jnp.stack(sel_p, axis=0)
    new_nsa_sel_sample = jnp.stack(sel_s, axis=0)
    new_nsa_win_prompt = jnp.stack(win_p, axis=0)
    new_nsa_win_sample = jnp.stack(win_s, axis=0)
    return (y_prompt, y_sample, new_diff_prompt, new_diff_sample, new_nsa_cmp_prompt, new_nsa_cmp_sample,
            new_nsa_sel_prompt, new_nsa_sel_sample, new_nsa_win_prompt, new_nsa_win_sample)
```

```python
import functools
import math

import jax
import jax.numpy as jnp
from jax import lax
from jax.experimental import pallas as pl
from jax.experimental.pallas import tpu as pltpu

F32 = jnp.float32
BF16 = jnp.bfloat16

D_MODEL = 2048
BATCH = 4
SEQ = 2048
DEPTH = 2
DEC_BATCH = 32
DEC_SEQ = 1
PAST_LEN = 8192
PAGE_SIZE = 128
HEAD_DIM = 128
ROPE_THETA = 10000.0
NORM_EPS = 1e-6
D_FF = 5632
DIFF_HEADS = D_MODEL // (2 * HEAD_DIM)
DIFF_KV_HEADS = 4
DIFF_REP = DIFF_HEADS // DIFF_KV_HEADS
NSA_HEADS = D_MODEL // HEAD_DIM
NSA_GROUPS = 4
NSA_REP = NSA_HEADS // NSA_GROUPS
CMP_LEN = 32
CMP_STRIDE = 16
CMP_HIDDEN = 256
SEL_LEN = 64
SEL_SHIFT = 6
SEL_TOPK = 16
WINDOW = 512
SEL_FORCE = 1e30

N_PAGES = PAST_LEN // PAGE_SIZE
LANES = 128
SUBLANES = 8
VMEM_LIMIT = 56 * 1024 * 1024

TM_DENSE = 1024
TM_FFN = 512
TF_FFN = 512
TN_DENSE = 512
TQ_DIFF = 256
TQ_NSA = 256
DIFF_PAGES_PER_STEP = 8
CMP_ROWS_PER_SLAB = 2048
CMP_CHUNKS_PER_SLAB = CMP_ROWS_PER_SLAB // CMP_STRIDE
CMP_PAGES_PER_SLAB = CMP_ROWS_PER_SLAB // PAGE_SIZE
N_SEL_PAST = PAST_LEN // SEL_LEN
J1_BATCH = 8
SEL_PAST_SLOTS = SEL_TOPK - 1


def _cparams(sem, vmem=VMEM_LIMIT):
    return pltpu.CompilerParams(dimension_semantics=sem, vmem_limit_bytes=vmem)


def _rms(x, g):
    ms = jnp.mean(x * x, axis=-1, keepdims=True)
    return x * lax.rsqrt(ms + NORM_EPS) * g


def _silu(x):
    return x * jax.nn.sigmoid(x)


def _dot(a, b):
    return jnp.dot(a, b, preferred_element_type=F32)


def _dot_nt(a, b):
    return lax.dot_general(a, b, (((1,), (1,)), ((), ())), preferred_element_type=F32)


def _masked_softmax(s, mask):
    s = jnp.where(mask, s, -jnp.inf)
    m = jnp.max(s, axis=-1, keepdims=True)
    m = jnp.where(jnp.isfinite(m), m, 0.0)
    e = jnp.exp(s - m)
    inv = 1.0 / jnp.maximum(jnp.sum(e, axis=-1, keepdims=True), 1e-30)
    return e * inv


def _ffn_kernel(x_ref, g_ref, wg_ref, wu_ref, wo_ref, gf_ref, o_ref, h_ref, *, n_j, final):
    j = pl.program_id(1)

    @pl.when(j == 0)
    def _():
        x = x_ref[...]
        h_ref[...] = _rms(x, g_ref[...]).astype(BF16)
        o_ref[...] = x

    h = h_ref[...]
    gate = _dot(h, wg_ref[...])
    up = _dot(h, wu_ref[...])
    a = (_silu(gate) * up).astype(BF16)
    o_ref[...] += 0.5 * _dot(a, wo_ref[...])

    if final:
        @pl.when(j == n_j - 1)
        def _():
            o_ref[...] = _rms(o_ref[...], gf_ref[...])


def _ffn(x, g, w_in, w_out, g_final, *, final):
    m = x.shape[0]
    tm = min(TM_FFN, m)
    n_j = D_FF // TF_FFN
    return pl.pallas_call(
        functools.partial(_ffn_kernel, n_j=n_j, final=final),
        out_shape=jax.ShapeDtypeStruct((m, D_MODEL), F32),
        grid=(m // tm, n_j),
        in_specs=[
            pl.BlockSpec((tm, D_MODEL), lambda i, j: (i, 0)),
            pl.BlockSpec((1, D_MODEL), lambda i, j: (0, 0)),
            pl.BlockSpec((D_MODEL, TF_FFN), lambda i, j: (0, j)),
            pl.BlockSpec((D_MODEL, TF_FFN), lambda i, j: (0, j + n_j)),
            pl.BlockSpec((TF_FFN, D_MODEL), lambda i, j: (j, 0)),
            pl.BlockSpec((1, D_MODEL), lambda i, j: (0, 0)),
        ],
        out_specs=pl.BlockSpec((tm, D_MODEL), lambda i, j: (i, 0)),
        scratch_shapes=[pltpu.VMEM((tm, D_MODEL), BF16)],
        compiler_params=_cparams(("parallel", "arbitrary")),
        name="ffn",
    )(x, g, w_in, w_in, w_out, g_final)


def _rope_cols(y, cos, sin):
    parts = []
    for hh in range(y.shape[1] // HEAD_DIM):
        yh = y[:, hh * HEAD_DIM:(hh + 1) * HEAD_DIM]
        parts.append(yh * cos + pltpu.roll(yh, HEAD_DIM // 2, 1) * sin)
    return parts[0] if len(parts) == 1 else jnp.concatenate(parts, axis=1)


def _proj_kernel(x_ref, g_ref, w_ref, cos_ref, sin_ref, o_ref, h_ref, *, mode, scale):
    j = pl.program_id(1)

    @pl.when(j == 0)
    def _():
        h_ref[...] = _rms(x_ref[...], g_ref[...]).astype(BF16)

    y = _dot(h_ref[...], w_ref[...])
    if mode == "rope":
        o_ref[...] = _rope_cols(y, cos_ref[...], sin_ref[...]) * scale
    elif mode == "rope_even":
        @pl.when(j % 2 == 0)
        def _():
            o_ref[...] = _rope_cols(y, cos_ref[...], sin_ref[...])

        @pl.when(j % 2 == 1)
        def _():
            o_ref[...] = y
    elif mode == "sigmoid":
        o_ref[...] = jax.nn.sigmoid(y)
    else:
        o_ref[...] = y


def _proj(x, g, w, col0, ncols, cos, sin, *, mode, scale=1.0, tn=TN_DENSE):
    m = x.shape[0]
    tm = min(TM_DENSE, m)
    tn = min(tn, ncols)
    jb0 = col0 // tn
    n_pos_tiles = cos.shape[0] // tm
    return pl.pallas_call(
        functools.partial(_proj_kernel, mode=mode, scale=scale),
        out_shape=jax.ShapeDtypeStruct((m, ncols), F32),
        grid=(m // tm, ncols // tn),
        in_specs=[
            pl.BlockSpec((tm, D_MODEL), lambda i, j: (i, 0)),
            pl.BlockSpec((1, D_MODEL), lambda i, j: (0, 0)),
            pl.BlockSpec((D_MODEL, tn), lambda i, j: (0, j + jb0)),
            pl.BlockSpec((tm, HEAD_DIM), lambda i, j: (i % n_pos_tiles, 0)),
            pl.BlockSpec((tm, HEAD_DIM), lambda i, j: (i % n_pos_tiles, 0)),
        ],
        out_specs=pl.BlockSpec((tm, tn), lambda i, j: (i, j)),
        scratch_shapes=[pltpu.VMEM((tm, D_MODEL), BF16)],
        compiler_params=_cparams(("parallel", "arbitrary")),
        name="proj_" + mode,
    )(x, g, w, cos, sin)


def _outproj_kernel(a_ref, w_ref, r_ref, o_ref):
    o_ref[...] = r_ref[...] + _dot(a_ref[...].astype(BF16), w_ref[...])


def _outproj(a, w, res):
    m, k = a.shape
    tm = min(TM_DENSE, m)
    tn = TN_DENSE
    return pl.pallas_call(
        _outproj_kernel,
        out_shape=jax.ShapeDtypeStruct((m, D_MODEL), F32),
        grid=(m // tm, D_MODEL // tn),
        in_specs=[
            pl.BlockSpec((tm, k), lambda i, j: (i, 0)),
            pl.BlockSpec((k, tn), lambda i, j: (0, j)),
            pl.BlockSpec((tm, tn), lambda i, j: (i, j)),
        ],
        out_specs=pl.BlockSpec((tm, tn), lambda i, j: (i, j)),
        compiler_params=_cparams(("parallel", "arbitrary")),
        name="outproj",
    )(a, w, res)


def _diff_lambda(lam, lam_init):
    a = jnp.sum(lam[0:1] * lam[1:2], axis=-1, keepdims=True)
    b = jnp.sum(lam[2:3] * lam[3:4], axis=-1, keepdims=True)
    return jnp.exp(a) - jnp.exp(b) + lam_init


def _subln(o, sub, lam_init):
    return _rms(o, sub) * (1.0 - lam_init)


def _diff_prompt_kernel(q_ref, k_ref, v_ref, lam_ref, sub_ref, o_ref, k1_s, k2_s, v_s, *, lam_init):
    k1_s[...] = k_ref[:, :HEAD_DIM].astype(BF16)
    k2_s[...] = k_ref[:, HEAD_DIM:].astype(BF16)
    v_s[...] = v_ref[...].astype(BF16)
    lmb = _diff_lambda(lam_ref[...], lam_init)
    sub = sub_ref[...]
    tq = TQ_DIFF
    hd2 = 2 * HEAD_DIM
    for qi in range(SEQ // tq):
        q0 = qi * tq
        kl = q0 + tq
        q1 = jnp.concatenate([q_ref[q0:q0 + tq, r * hd2:r * hd2 + HEAD_DIM]
                              for r in range(DIFF_REP)], axis=0).astype(BF16)
        q2 = jnp.concatenate([q_ref[q0:q0 + tq, r * hd2 + HEAD_DIM:(r + 1) * hd2]
                              for r in range(DIFF_REP)], axis=0).astype(BF16)
        rows = DIFF_REP * tq
        pos_q = q0 + (lax.broadcasted_iota(jnp.int32, (rows, kl), 0) & (tq - 1))
        mask = lax.broadcasted_iota(jnp.int32, (rows, kl), 1) <= pos_q
        p1 = _masked_softmax(_dot_nt(q1, k1_s[:kl]), mask)
        p2 = _masked_softmax(_dot_nt(q2, k2_s[:kl]), mask)
        a = (p1 - lmb * p2).astype(BF16)
        o = _subln(_dot(a, v_s[:kl]), sub, lam_init)
        for r in range(DIFF_REP):
            o_ref[q0:q0 + tq, r * hd2:(r + 1) * hd2] = o[r * tq:(r + 1) * tq].astype(BF16)


def _diff_prompt(q, kv, lam, sub, lam_init):
    gw = DIFF_REP * 2 * HEAD_DIM
    return pl.pallas_call(
        functools.partial(_diff_prompt_kernel, lam_init=lam_init),
        out_shape=jax.ShapeDtypeStruct((BATCH * SEQ, D_MODEL), BF16),
        grid=(BATCH, DIFF_KV_HEADS),
        in_specs=[
            pl.BlockSpec((SEQ, gw), lambda b, g: (b, g)),
            pl.BlockSpec((SEQ, 2 * HEAD_DIM), lambda b, g: (b, g)),
            pl.BlockSpec((SEQ, 2 * HEAD_DIM), lambda b, g: (b, DIFF_KV_HEADS + g)),
            pl.BlockSpec((4, HEAD_DIM), lambda b, g: (0, 0)),
            pl.BlockSpec((1, 2 * HEAD_DIM), lambda b, g: (0, 0)),
        ],
        out_specs=pl.BlockSpec((SEQ, gw), lambda b, g: (b, g)),
        scratch_shapes=[pltpu.VMEM((SEQ, HEAD_DIM), BF16), pltpu.VMEM((SEQ, HEAD_DIM), BF16),
                        pltpu.VMEM((SEQ, 2 * HEAD_DIM), BF16)],
        compiler_params=_cparams(("parallel", "parallel")),
        name="diff_prompt",
    )(q, kv, kv, lam, sub)


def _diff_sample_kernel(pt_ref, q_ref, kvn_ref, *rest, lam_init, n_steps):
    del pt_ref
    pages = rest[:DIFF_PAGES_PER_STEP]
    lam_ref, sub_ref, o_ref, m_s, l_s, acc_s = rest[DIFF_PAGES_PER_STEP:]
    step = pl.program_id(1)
    hd2 = 2 * HEAD_DIM
    kv_off = DIFF_KV_HEADS * hd2

    @pl.when(step == 0)
    def _():
        m_s[...] = jnp.full(m_s.shape, -jnp.inf, F32)
        l_s[...] = jnp.zeros(l_s.shape, F32)
        acc_s[...] = jnp.zeros(acc_s.shape, F32)

    row = lax.broadcasted_iota(jnp.int32, (SUBLANES, hd2), 0)
    lane = lax.broadcasted_iota(jnp.int32, (SUBLANES, hd2), 1)
    q = q_ref[...]
    qmats = []
    for g in range(DIFF_KV_HEADS):
        qg = jnp.zeros((SUBLANES, hd2), F32)
        for r in range(DIFF_REP):
            qr = q[:, (g * DIFF_REP + r) * hd2:(g * DIFF_REP + r + 1) * hd2]
            qg = qg + jnp.where((row == r) & (lane < HEAD_DIM), qr, 0.0)
            qg = qg + jnp.where((row == DIFF_REP + r) & (lane >= HEAD_DIM), qr, 0.0)
        qmats.append(qg)

    for g in range(DIFF_KV_HEADS):
        qb = qmats[g].astype(BF16)
        s_list = [_dot_nt(qb, pg[:, g * hd2:(g + 1) * hd2].astype(BF16)) for pg in pages]
        m_prev = m_s[g][:, :1]
        m_blk = functools.reduce(jnp.maximum, s_list)
        m_new = jnp.maximum(m_prev, jnp.max(m_blk, axis=-1, keepdims=True))
        corr = jnp.exp(m_prev - m_new)
        l_new = corr * l_s[g][:, :1]
        acc = corr * acc_s[g]
        for s, pg in zip(s_list, pages):
            p = jnp.exp(s - m_new)
            l_new = l_new + jnp.sum(p, axis=-1, keepdims=True)
            acc = acc + _dot(p.astype(BF16), pg[:, kv_off + g * hd2:kv_off + (g + 1) * hd2].astype(BF16))
        m_s[g] = jnp.broadcast_to(m_new, (SUBLANES, LANES))
        l_s[g] = jnp.broadcast_to(l_new, (SUBLANES, LANES))
        acc_s[g] = acc

    @pl.when(step == n_steps - 1)
    def _():
        lmb = _diff_lambda(lam_ref[...], lam_init)
        sub = sub_ref[...]
        kvn = kvn_ref[...]
        for g in range(DIFF_KV_HEADS):
            k_new = kvn[:, g * hd2:(g + 1) * hd2]
            v_new = kvn[:, kv_off + g * hd2:kv_off + (g + 1) * hd2]
            s_new = jnp.sum(qmats[g] * k_new, axis=-1, keepdims=True)
            m_prev = m_s[g][:, :1]
            m_new = jnp.maximum(m_prev, s_new)
            corr = jnp.exp(m_prev - m_new)
            p_new = jnp.exp(s_new - m_new)
            den = corr * l_s[g][:, :1] + p_new
            acc = corr * acc_s[g] + p_new * v_new
            on = acc * (1.0 / den)
            for r in range(DIFF_REP):
                o = on[r:r + 1] - lmb * on[DIFF_REP + r:DIFF_REP + r + 1]
                h = g * DIFF_REP + r
                o_ref[:, h * hd2:(h + 1) * hd2] = _subln(o, sub, lam_init)


def _diff_sample(q, kvn, cache, layer, page_table, lam, sub, lam_init):
    n_steps = N_PAGES // DIFF_PAGES_PER_STEP
    hd2 = 2 * HEAD_DIM
    page_specs = [
        pl.BlockSpec((None, None, PAGE_SIZE, D_MODEL),
                     functools.partial(lambda b, s, pt, k: (layer, pt[b, s * DIFF_PAGES_PER_STEP + k], 0, 0), k=k))
        for k in range(DIFF_PAGES_PER_STEP)
    ]
    grid_spec = pltpu.PrefetchScalarGridSpec(
        num_scalar_prefetch=1,
        grid=(DEC_BATCH, n_steps),
        in_specs=[
            pl.BlockSpec((None, 1, D_MODEL), lambda b, s, pt: (b, 0, 0)),
            pl.BlockSpec((None, 1, D_MODEL), lambda b, s, pt: (b, 0, 0)),
            *page_specs,
            pl.BlockSpec((4, HEAD_DIM), lambda b, s, pt: (0, 0)),
            pl.BlockSpec((1, hd2), lambda b, s, pt: (0, 0)),
        ],
        out_specs=pl.BlockSpec((None, 1, D_MODEL), lambda b, s, pt: (b, 0, 0)),
        scratch_shapes=[pltpu.VMEM((DIFF_KV_HEADS, SUBLANES, LANES), F32),
                        pltpu.VMEM((DIFF_KV_HEADS, SUBLANES, LANES), F32),
                        pltpu.VMEM((DIFF_KV_HEADS, SUBLANES, hd2), F32)],
    )
    return pl.pallas_call(
        functools.partial(_diff_sample_kernel, lam_init=lam_init, n_steps=n_steps),
        out_shape=jax.ShapeDtypeStruct((DEC_BATCH, 1, D_MODEL), F32),
        grid_spec=grid_spec,
        compiler_params=_cparams(("parallel", "arbitrary")),
        name="diff_sample",
    )(page_table, q, kvn, *([cache] * DIFF_PAGES_PER_STEP), lam, sub)


def _compress_kernel(*refs, n_src, n_prefetch):
    refs = refs[n_prefetch:]
    srcs = refs[:n_src]
    wcat_ref, pe_ref, b1_ref, w2_ref, o_ref, x_s, carry_s = refs[n_src:]
    step = pl.program_id(1)

    @pl.when(step == 0)
    def _():
        carry_s[...] = jnp.zeros(carry_s.shape, F32)

    chunks_per_src = CMP_CHUNKS_PER_SLAB // n_src
    planes = 2 * NSA_GROUPS
    nc = CMP_CHUNKS_PER_SLAB
    row0 = lax.broadcasted_iota(jnp.int32, (nc, CMP_HIDDEN), 0) == 0
    for n in range(2):
        for g in range(NSA_GROUPS):
            plane = n * NSA_GROUPS + g
            for r in range(CMP_STRIDE):
                pieces = [src[pl.ds(r * planes + plane, chunks_per_src, stride=CMP_STRIDE * planes), :]
                          for src in srcs]
                blk = pieces[0] if n_src == 1 else jnp.concatenate(pieces, axis=0)
                x_s[g * nc:(g + 1) * nc, r * HEAD_DIM:(r + 1) * HEAD_DIM] = blk.astype(BF16)
        wcat = wcat_ref[n]
        fs = _dot(x_s[...], wcat)
        pb = _dot(pe_ref[n].astype(BF16), wcat)
        bias = pb[0:1, :CMP_HIDDEN] + pb[1:2, CMP_HIDDEN:] + b1_ref[n]
        for g in range(NSA_GROUPS):
            first = fs[g * nc:(g + 1) * nc, :CMP_HIDDEN]
            second = fs[g * nc:(g + 1) * nc, CMP_HIDDEN:]
            prev_first = jnp.where(row0, carry_s[n, g][0:1], pltpu.roll(first, 1, 0))
            carry_s[n, g] = jnp.broadcast_to(first[nc - 1:nc], (SUBLANES, CMP_HIDDEN))
            hid = _silu(prev_first + second + bias)
            o_ref[n, g] = _dot(hid.astype(BF16), w2_ref[n])


def _compress_specs(idx):
    return [
        pl.BlockSpec((2, CMP_STRIDE * HEAD_DIM, 2 * CMP_HIDDEN), idx(lambda: (0, 0, 0))),
        pl.BlockSpec((2, SUBLANES, CMP_STRIDE * HEAD_DIM), idx(lambda: (0, 0, 0))),
        pl.BlockSpec((2, 1, CMP_HIDDEN), idx(lambda: (0, 0, 0))),
        pl.BlockSpec((2, CMP_HIDDEN, HEAD_DIM), idx(lambda: (0, 0, 0))),
    ]


_CMP_SCRATCH = [pltpu.VMEM((NSA_GROUPS * CMP_CHUNKS_PER_SLAB, CMP_STRIDE * HEAD_DIM), BF16),
                pltpu.VMEM((2, NSA_GROUPS, SUBLANES, CMP_HIDDEN), F32)]


def _compress_prompt(rows, cw):
    n_slab = SEQ // CMP_ROWS_PER_SLAB
    planes = 2 * NSA_GROUPS
    return pl.pallas_call(
        functools.partial(_compress_kernel, n_src=1, n_prefetch=0),
        out_shape=jax.ShapeDtypeStruct((BATCH, 2, NSA_GROUPS, n_slab * CMP_CHUNKS_PER_SLAB, HEAD_DIM), F32),
        grid=(BATCH, n_slab),
        in_specs=[pl.BlockSpec((CMP_ROWS_PER_SLAB * planes, HEAD_DIM), lambda b, s: (b * n_slab + s, 0))]
        + _compress_specs(lambda f: (lambda b, s: f())),
        out_specs=pl.BlockSpec((None, 2, NSA_GROUPS, CMP_CHUNKS_PER_SLAB, HEAD_DIM), lambda b, s: (b, 0, 0, s, 0)),
        scratch_shapes=_CMP_SCRATCH,
        compiler_params=_cparams(("parallel", "arbitrary")),
        name="compress_prompt",
    )(rows, *cw)


def _compress_sample(cache, layer, page_table, cw):
    n_slab = PAST_LEN // CMP_ROWS_PER_SLAB
    planes = 2 * NSA_GROUPS
    page_specs = [
        pl.BlockSpec((None, None, PAGE_SIZE * planes, HEAD_DIM),
                     functools.partial(lambda b, s, pt, k: (layer, pt[b, s * CMP_PAGES_PER_SLAB + k], 0, 0), k=k))
        for k in range(CMP_PAGES_PER_SLAB)
    ]
    grid_spec = pltpu.PrefetchScalarGridSpec(
        num_scalar_prefetch=1,
        grid=(DEC_BATCH, n_slab),
        in_specs=page_specs + _compress_specs(lambda f: (lambda b, s, pt: f())),
        out_specs=pl.BlockSpec((None, 2, NSA_GROUPS, CMP_CHUNKS_PER_SLAB, HEAD_DIM),
                               lambda b, s, pt: (b, 0, 0, s, 0)),
        scratch_shapes=_CMP_SCRATCH,
    )
    return pl.pallas_call(
        functools.partial(_compress_kernel, n_src=CMP_PAGES_PER_SLAB, n_prefetch=1),
        out_shape=jax.ShapeDtypeStruct((DEC_BATCH, 2, NSA_GROUPS, n_slab * CMP_CHUNKS_PER_SLAB, HEAD_DIM), F32),
        grid_spec=grid_spec,
        compiler_params=_cparams(("parallel", "arbitrary")),
        name="compress_sample",
    )(page_table, *([cache] * CMP_PAGES_PER_SLAB), *cw)


def _topk_rank(score, blk, n_blocks):
    rank = jnp.zeros(score.shape, F32)
    for sp in range(n_blocks):
        c = score[:, sp:sp + 1]
        before = (c > score) | ((c == score) & (blk > sp))
        rank = rank + jnp.where(before, 1.0, 0.0)
    return rank


def _nsa_prompt_kernel(q_ref, kvc_ref, ks_ref, vs_ref, kw_ref, vw_ref, gt_ref, cover_ref, expand_ref,
                       o_ref, ks_s, vs_s, kw_s, vw_s):
    ks_s[...] = ks_ref[...].astype(BF16)
    vs_s[...] = vs_ref[...].astype(BF16)
    kw_s[...] = kw_ref[...].astype(BF16)
    vw_s[...] = vw_ref[...].astype(BF16)
    kc = kvc_ref[0].astype(BF16)
    vc = kvc_ref[1].astype(BF16)
    n_ent = kc.shape[0]
    tq = TQ_NSA
    rows = NSA_REP * tq
    for qi in range(SEQ // tq):
        q0 = qi * tq
        kl = q0 + tq
        qb = jnp.concatenate([q_ref[q0:q0 + tq, r * HEAD_DIM:(r + 1) * HEAD_DIM]
                              for r in range(NSA_REP)], axis=0).astype(BF16)

        pos_c = q0 + (lax.broadcasted_iota(jnp.int32, (rows, n_ent), 0) & (tq - 1))
        ent = lax.broadcasted_iota(jnp.int32, (rows, n_ent), 1)
        mask_c = (ent >= 1) & (ent * CMP_STRIDE + (CMP_LEN - 1 - CMP_STRIDE) <= pos_c)
        p_c = _masked_softmax(_dot_nt(qb, kc), mask_c)
        o_c = _dot(p_c.astype(BF16), vc)

        psum = functools.reduce(lambda a, b: a + b, [p_c[r * tq:(r + 1) * tq] for r in range(NSA_REP)])
        imp = _dot(psum.astype(BF16), cover_ref[...])
        blk = lax.broadcasted_iota(jnp.int32, (tq, LANES), 1)
        cur = (q0 + lax.broadcasted_iota(jnp.int32, (tq, LANES), 0)) >> SEL_SHIFT
        n_blk = kl // SEL_LEN
        if n_blk <= SEL_TOPK:
            sel = jnp.where(blk <= cur, 1.0, 0.0)
        else:
            forced = (blk == 0) | (blk == cur) | (blk == cur - 1)
            score = jnp.where(forced, SEL_FORCE, jnp.where(blk <= cur, imp, -SEL_FORCE))
            rank = _topk_rank(score, blk, n_blk)
            sel = jnp.where((rank < SEL_TOPK) & (blk <= cur), 1.0, 0.0)
        selk = _dot(sel.astype(BF16), expand_ref[:, :kl])
        selk = jnp.concatenate([selk] * NSA_REP, axis=0)
        pos_s = q0 + (lax.broadcasted_iota(jnp.int32, (rows, kl), 0) & (tq - 1))
        key_s = lax.broadcasted_iota(jnp.int32, (rows, kl), 1)
        mask_s = (selk > 0.5) & (key_s <= pos_s)
        p_s = _masked_softmax(_dot_nt(qb, ks_s[:kl]), mask_s)
        o_s = _dot(p_s.astype(BF16), vs_s[:kl])

        w0 = max(0, q0 - WINDOW)
        wl = kl - w0
        rel = (q0 - w0) + (lax.broadcasted_iota(jnp.int32, (rows, wl), 0) & (tq - 1)) \
            - lax.broadcasted_iota(jnp.int32, (rows, wl), 1)
        mask_w = (rel >= 0) & (rel < WINDOW)
        p_w = _masked_softmax(_dot_nt(qb, kw_s[w0:kl]), mask_w)
        o_w = _dot(p_w.astype(BF16), vw_s[w0:kl])

        gt = gt_ref[q0:q0 + tq, :]
        for r in range(NSA_REP):
            sl = slice(r * tq, (r + 1) * tq)
            o = (gt[:, 3 * r:3 * r + 1] * o_c[sl] + gt[:, 3 * r + 1:3 * r + 2] * o_s[sl]
                 + gt[:, 3 * r + 2:3 * r + 3] * o_w[sl])
            o_ref[q0:q0 + tq, r * HEAD_DIM:(r + 1) * HEAD_DIM] = o.astype(BF16)


def _nsa_prompt(q, kvc, sel_rows, win_rows, gates, cover, expand):
    gw = NSA_REP * HEAD_DIM
    n_ent = kvc.shape[3]
    row_spec_k = pl.BlockSpec((SEQ, HEAD_DIM), lambda b, g: (b, g))
    row_spec_v = pl.BlockSpec((SEQ, HEAD_DIM), lambda b, g: (b, NSA_GROUPS + g))
    return pl.pallas_call(
        _nsa_prompt_kernel,
        out_shape=jax.ShapeDtypeStruct((BATCH * SEQ, D_MODEL), BF16),
        grid=(BATCH, NSA_GROUPS),
        in_specs=[
            pl.BlockSpec((SEQ, gw), lambda b, g: (b, g)),
            pl.BlockSpec((None, 2, None, n_ent, HEAD_DIM), lambda b, g: (b, 0, g, 0, 0)),
            row_spec_k, row_spec_v, row_spec_k, row_spec_v,
            pl.BlockSpec((SEQ, LANES), lambda b, g: (b, g)),
            pl.BlockSpec(cover.shape, lambda b, g: (0, 0)),
            pl.BlockSpec(expand.shape, lambda b, g: (0, 0)),
        ],
        out_specs=pl.BlockSpec((SEQ, gw), lambda b, g: (b, g)),
        scratch_shapes=[pltpu.VMEM((SEQ, HEAD_DIM), BF16)] * 4,
        compiler_params=_cparams(("parallel", "parallel")),
        name="nsa_prompt",
    )(q, kvc, sel_rows, sel_rows, win_rows, win_rows, gates, cover, expand)


def _head_rows(q):
    parts = [q[:, r * HEAD_DIM:(r + 1) * HEAD_DIM] for r in range(NSA_REP)]
    parts.append(jnp.zeros((SUBLANES - NSA_REP, HEAD_DIM), F32))
    return jnp.concatenate(parts, axis=0)


def _nsa_sample_cmp_kernel(q_ref, kvc_ref, cover_ref, o_ref, idx_ref):
    n_ent = kvc_ref.shape[2]
    ent = lax.broadcasted_iota(jnp.int32, (SUBLANES, n_ent), 1)
    mask_c = ent >= 1
    psums = []
    for bi in range(J1_BATCH):
        qh = _head_rows(q_ref[bi:bi + 1, :])
        p = _masked_softmax(_dot_nt(qh.astype(BF16), kvc_ref[bi, 0].astype(BF16)), mask_c)
        o = _dot(p.astype(BF16), kvc_ref[bi, 1].astype(BF16))
        for r in range(NSA_REP):
            o_ref[bi:bi + 1, r * HEAD_DIM:(r + 1) * HEAD_DIM] = o[r:r + 1]
        psums.append(functools.reduce(lambda a, b: a + b, [p[r:r + 1] for r in range(NSA_REP)]))
    psum = jnp.concatenate(psums, axis=0)
    imp = _dot(psum.astype(BF16), cover_ref[...])
    blk = lax.broadcasted_iota(jnp.int32, imp.shape, 1)
    cur = N_SEL_PAST
    forced = (blk == 0) | (blk == cur) | (blk == cur - 1)
    score = jnp.where(forced, SEL_FORCE, jnp.where(blk <= cur, imp, -SEL_FORCE))
    rank = _topk_rank(score, blk, cur + 1)
    blk_f = blk.astype(F32)
    lane = lax.broadcasted_iota(jnp.int32, (J1_BATCH, LANES), 1)
    out = jnp.zeros((J1_BATCH, LANES), F32)
    for k in range(SEL_TOPK):
        ik = jnp.sum(jnp.where(rank == float(k), blk_f, 0.0), axis=-1, keepdims=True)
        out = out + jnp.where(lane == k, ik, 0.0)
    idx_ref[...] = out.astype(jnp.int32)


def _nsa_sample_cmp(q, kvc, cover):
    gw = NSA_REP * HEAD_DIM
    n_ent = kvc.shape[3]
    return pl.pallas_call(
        _nsa_sample_cmp_kernel,
        out_shape=(jax.ShapeDtypeStruct((DEC_BATCH, D_MODEL), F32),
                   jax.ShapeDtypeStruct((NSA_GROUPS, DEC_BATCH, LANES), jnp.int32)),
        grid=(NSA_GROUPS, DEC_BATCH // J1_BATCH),
        in_specs=[
            pl.BlockSpec((J1_BATCH, gw), lambda g, c: (c, g)),
            pl.BlockSpec((J1_BATCH, 2, None, n_ent, HEAD_DIM), lambda g, c: (c, 0, g, 0, 0)),
            pl.BlockSpec(cover.shape, lambda g, c: (0, 0)),
        ],
        out_specs=(pl.BlockSpec((J1_BATCH, gw), lambda g, c: (c, g)),
                   pl.BlockSpec((None, J1_BATCH, LANES), lambda g, c: (g, c, 0))),
        compiler_params=_cparams(("parallel", "parallel")),
        name="nsa_sample_cmp",
    )(q, kvc, cover)


def _nsa_sample_kernel(pt_ref, idx_ref, q_ref, oc_ref, gt_ref, ksn_ref, vsn_ref, kwn_ref, vwn_ref,
                       kw_ref, vw_ref, *rest):
    del pt_ref, idx_ref
    kblocks = rest[:SEL_PAST_SLOTS]
    vblocks = rest[SEL_PAST_SLOTS:2 * SEL_PAST_SLOTS]
    o_ref = rest[2 * SEL_PAST_SLOTS]
    qh = _head_rows(q_ref[...])
    qb = qh.astype(BF16)

    s_list = [_dot_nt(qb, kb[...].astype(BF16)) for kb in kblocks]
    s_new = jnp.sum(qh * ksn_ref[...], axis=-1, keepdims=True)
    m = jnp.maximum(jnp.max(functools.reduce(jnp.maximum, s_list), axis=-1, keepdims=True), s_new)
    e_new = jnp.exp(s_new - m)
    den = e_new
    acc = e_new * vsn_ref[...]
    for s, vb in zip(s_list, vblocks):
        e = jnp.exp(s - m)
        den = den + jnp.sum(e, axis=-1, keepdims=True)
        acc = acc + _dot(e.astype(BF16), vb[...].astype(BF16))
    o_s = acc * (1.0 / jnp.maximum(den, 1e-30))

    wb = kw_ref.shape[0]
    s_w = _dot_nt(qb, kw_ref[...].astype(BF16))
    mask_w = lax.broadcasted_iota(jnp.int32, (SUBLANES, wb), 1) >= 1
    s_w = jnp.where(mask_w, s_w, -jnp.inf)
    s_wn = jnp.sum(qh * kwn_ref[...], axis=-1, keepdims=True)
    m_w = jnp.maximum(jnp.max(s_w, axis=-1, keepdims=True), s_wn)
    e_w = jnp.exp(s_w - m_w)
    e_wn = jnp.exp(s_wn - m_w)
    den_w = jnp.sum(e_w, axis=-1, keepdims=True) + e_wn
    o_w = (_dot(e_w.astype(BF16), vw_ref[...].astype(BF16)) + e_wn * vwn_ref[...]) * (1.0 / jnp.maximum(den_w, 1e-30))

    gt = gt_ref[...]
    for r in range(NSA_REP):
        o = (gt[:, 3 * r:3 * r + 1] * oc_ref[:, r * HEAD_DIM:(r + 1) * HEAD_DIM]
             + gt[:, 3 * r + 1:3 * r + 2] * o_s[r:r + 1] + gt[:, 3 * r + 2:3 * r + 3] * o_w[r:r + 1])
        o_ref[:, r * HEAD_DIM:(r + 1) * HEAD_DIM] = o


def _nsa_sample(q, o_c, gates, sel_new, win_new, win_state, cache_sel, layer, page_table, idx):
    gw = NSA_REP * HEAD_DIM
    wb = win_state.shape[2]
    sub = PAGE_SIZE // SEL_LEN
    slots = [0, 1] + list(range(3, SEL_TOPK))

    def half_page(b, g, pt, ix, slot):
        blk = jnp.minimum(ix[b, g * SEL_TOPK + slot], N_SEL_PAST - 1)
        return pt[b, blk // sub] * sub + blk % sub

    def kspec(slot, voff):
        return pl.BlockSpec((None, None, SEL_LEN, HEAD_DIM),
                            lambda b, g, pt, ix: (layer, half_page(b, g, pt, ix, slot), 0, voff + g))

    row = lambda voff: pl.BlockSpec((None, 1, HEAD_DIM), lambda b, g, pt, ix: (b, 0, voff + g))
    grid_spec = pltpu.PrefetchScalarGridSpec(
        num_scalar_prefetch=2,
        grid=(DEC_BATCH, NSA_GROUPS),
        in_specs=[
            pl.BlockSpec((None, 1, gw), lambda b, g, pt, ix: (b, 0, g)),
            pl.BlockSpec((None, 1, gw), lambda b, g, pt, ix: (b, 0, g)),
            pl.BlockSpec((None, 1, LANES), lambda b, g, pt, ix: (b, 0, g)),
            row(0), row(NSA_GROUPS), row(0), row(NSA_GROUPS),
            pl.BlockSpec((None, None, wb, HEAD_DIM), lambda b, g, pt, ix: (layer, b, 0, g)),
            pl.BlockSpec((None, None, wb, HEAD_DIM), lambda b, g, pt, ix: (layer, b, 0, NSA_GROUPS + g)),
            *[kspec(s, 0) for s in slots],
            *[kspec(s, NSA_GROUPS) for s in slots],
        ],
        out_specs=pl.BlockSpec((None, 1, gw), lambda b, g, pt, ix: (b, 0, g)),
    )
    return pl.pallas_call(
        _nsa_sample_kernel,
        out_shape=jax.ShapeDtypeStruct((DEC_BATCH, 1, D_MODEL), F32),
        grid_spec=grid_spec,
        compiler_params=_cparams(("parallel", "parallel")),
        name="nsa_sample",
    )(page_table, idx, q, o_c, gates, sel_new, sel_new, win_new, win_new, win_state, win_state,
      *([cache_sel] * (2 * SEL_PAST_SLOTS)))


def _rope_tables(pos):
    half = HEAD_DIM // 2
    inv = ROPE_THETA ** (-jnp.arange(half, dtype=F32) / half)
    ang = pos.astype(F32)[:, None] * inv[None, :]
    cos, sin = jnp.cos(ang), jnp.sin(ang)
    return jnp.concatenate([cos, cos], axis=-1), jnp.concatenate([-sin, sin], axis=-1)


def _cover_table(n_ent, n_sel, n_lanes):
    start = (jnp.arange(n_ent)[:, None] - 1) * CMP_STRIDE
    bstart = jnp.arange(n_lanes)[None, :] * SEL_LEN
    hit = (start < bstart + SEL_LEN) & (start + CMP_LEN > bstart)
    hit = hit & (jnp.arange(n_ent)[:, None] >= 1) & (jnp.arange(n_lanes)[None, :] < n_sel)
    return hit.astype(BF16)


def _expand_table(n_keys):
    return (jnp.arange(LANES)[:, None] == (jnp.arange(n_keys)[None, :] // SEL_LEN)).astype(BF16)


def kernel(x_prompt, x_sample, cache_diff, cache_nsa_cmp, cache_nsa_sel, state_nsa_win, page_table,
           norm_ffn1, w_ffn1_in, w_ffn1_out, norm_mix, norm_ffn2, w_ffn2_in, w_ffn2_out,
           w_diff_qkv, diff_lambda, diff_subln, w_diff_o,
           w_nsa_in, nsa_cmp_pe, nsa_cmp_w1, nsa_cmp_b1, nsa_cmp_w2, w_nsa_o, norm_final):
    xp = x_prompt.reshape(BATCH * SEQ, D_MODEL)
    xs = x_sample.reshape(DEC_BATCH * DEC_SEQ, D_MODEL)
    n_pool = cache_diff.shape[1]
    wb = state_nsa_win.shape[2]
    gw2 = 2 * NSA_GROUPS * HEAD_DIM
    cos_p, sin_p = _rope_tables(jnp.arange(SEQ))
    cos_s, sin_s = _rope_tables(jnp.full((DEC_BATCH,), PAST_LEN))
    qscale = HEAD_DIM ** -0.5
    row2 = lambda v: v.reshape(1, -1)

    outs = {k: [] for k in ("diff_p", "diff_s", "cmp_p", "cmp_s", "sel_p", "sel_s", "win_p", "win_s")}
    for i in range(DEPTH):
        last = i == DEPTH - 1
        layer = i // 2
        xp = _ffn(xp, row2(norm_ffn1[i]), w_ffn1_in[i].astype(BF16), w_ffn1_out[i].astype(BF16),
                  row2(norm_final), final=False)
        xs = _ffn(xs, row2(norm_ffn1[i]), w_ffn1_in[i].astype(BF16), w_ffn1_out[i].astype(BF16),
                  row2(norm_final), final=False)
        gmix = row2(norm_mix[i])
        if i % 2 == 0:
            lam_init = 0.8 - 0.6 * math.exp(-0.3 * i)
            w = w_diff_qkv[layer].astype(BF16)
            qd = 2 * DIFF_HEADS * HEAD_DIM
            kd = 2 * DIFF_KV_HEADS * HEAD_DIM
            lam = diff_lambda[layer]
            sub = row2(diff_subln[layer])
            wo = w_diff_o[layer].astype(BF16)
            res = []
            for x, cos, sin in ((xp, cos_p, sin_p), (xs, cos_s, sin_s)):
                q = _proj(x, gmix, w, 0, qd, cos, sin, mode="rope", scale=qscale)
                kv = _proj(x, gmix, w, qd, 2 * kd, cos, sin, mode="rope_even", tn=kd)
                res.append((q, kv))
            (qp, kvp), (qs, kvs) = res
            op = _diff_prompt(qp, kvp, lam, sub, lam_init)
            cache = cache_diff.reshape(cache_diff.shape[0], n_pool, PAGE_SIZE, D_MODEL)
            os_ = _diff_sample(qs.reshape(DEC_BATCH, 1, D_MODEL), kvs.reshape(DEC_BATCH, 1, D_MODEL), cache, layer,
                               page_table, lam, sub, lam_init).reshape(DEC_BATCH, D_MODEL)
            xp = _outproj(op, wo, xp)
            xs = _outproj(os_, wo, xs)
            outs["diff_p"].append(kvp.reshape(BATCH, SEQ, 2, DIFF_KV_HEADS, 2 * HEAD_DIM))
            outs["diff_s"].append(kvs.reshape(DEC_BATCH, DEC_SEQ, 2, DIFF_KV_HEADS, 2 * HEAD_DIM))
        else:
            w_in = w_nsa_in[layer]
            w = w_in.astype(BF16)
            qd = NSA_HEADS * HEAD_DIM
            wg = w_in[:, qd + 3 * gw2:].reshape(D_MODEL, NSA_GROUPS, NSA_REP * 3)
            wg = jnp.pad(wg, ((0, 0), (0, 0), (0, LANES - NSA_REP * 3))).reshape(D_MODEL, NSA_GROUPS * LANES)
            wg = wg.astype(BF16)
            wo = w_nsa_o[layer].astype(BF16)
            w1 = nsa_cmp_w1[layer]
            wcat = jnp.concatenate([w1[:, :CMP_STRIDE].reshape(2, CMP_STRIDE * HEAD_DIM, CMP_HIDDEN),
                                    w1[:, CMP_STRIDE:].reshape(2, CMP_STRIDE * HEAD_DIM, CMP_HIDDEN)],
                                   axis=-1).astype(BF16)
            pe = nsa_cmp_pe[layer].reshape(2, 2, CMP_STRIDE * HEAD_DIM)
            pe = jnp.pad(pe, ((0, 0), (0, SUBLANES - 2), (0, 0)))
            cw = (wcat, pe, nsa_cmp_b1[layer].reshape(2, 1, CMP_HIDDEN), nsa_cmp_w2[layer].astype(BF16))
            res = []
            for x, cos, sin in ((xp, cos_p, sin_p), (xs, cos_s, sin_s)):
                q = _proj(x, gmix, w, 0, qd, cos, sin, mode="rope", scale=qscale)
                rows = [_proj(x, gmix, w, qd + c * gw2, gw2, cos, sin, mode="rope_even") for c in range(3)]
                gates = _proj(x, gmix, wg, 0, NSA_GROUPS * LANES, cos, sin, mode="sigmoid")
                res.append((q, rows, gates))
            (qp, rows_p, gates_p), (qs, rows_s, gates_s) = res

            kvc_p = _compress_prompt(rows_p[0].reshape(BATCH * SEQ * 2 * NSA_GROUPS, HEAD_DIM), cw)
            n_ent_p = kvc_p.shape[3]
            op = _nsa_prompt(qp, kvc_p, rows_p[1], rows_p[2], gates_p,
                             _cover_table(n_ent_p, SEQ // SEL_LEN, LANES), _expand_table(SEQ))

            cache_c = cache_nsa_cmp.reshape(cache_nsa_cmp.shape[0], n_pool, PAGE_SIZE * 2 * NSA_GROUPS, HEAD_DIM)
            kvc_s = _compress_sample(cache_c, layer, page_table, cw)
            o_c, idx = _nsa_sample_cmp(qs, kvc_s, _cover_table(kvc_s.shape[3], N_SEL_PAST + 1, 2 * LANES))
            idx = jnp.transpose(idx[:, :, :SEL_TOPK], (1, 0, 2)).reshape(DEC_BATCH, NSA_GROUPS * SEL_TOPK)
            cache_s = cache_nsa_sel.reshape(cache_nsa_sel.shape[0], n_pool * (PAGE_SIZE // SEL_LEN), SEL_LEN, gw2)
            win_state = state_nsa_win.reshape(state_nsa_win.shape[0], DEC_BATCH, wb, gw2)
            r3 = lambda a: a.reshape(DEC_BATCH, 1, a.shape[-1])
            os_ = _nsa_sample(r3(qs), r3(o_c), r3(gates_s), r3(rows_s[1]), r3(rows_s[2]), win_state, cache_s,
                              layer, page_table, idx).reshape(DEC_BATCH, D_MODEL)
            xp = _outproj(op, wo, xp)
            xs = _outproj(os_, wo, xs)
            shp = (2, NSA_GROUPS, HEAD_DIM)
            outs["cmp_p"].append(rows_p[0].reshape(BATCH, SEQ, *shp))
            outs["cmp_s"].append(rows_s[0].reshape(DEC_BATCH, DEC_SEQ, *shp))
            outs["sel_p"].append(rows_p[1].reshape(BATCH, SEQ, *shp))
            outs["sel_s"].append(rows_s[1].reshape(DEC_BATCH, DEC_SEQ, *shp))
            wp = rows_p[2].reshape(BATCH, SEQ, *shp)
            outs["win_p"].append(wp[:, SEQ - min(WINDOW, SEQ):])
            kvw = jnp.concatenate([state_nsa_win[layer], rows_s[2].reshape(DEC_BATCH, DEC_SEQ, *shp)], axis=1)
            outs["win_s"].append(kvw[:, DEC_SEQ:])
        xp = _ffn(xp, row2(norm_ffn2[i]), w_ffn2_in[i].astype(BF16), w_ffn2_out[i].astype(BF16),
                  row2(norm_final), final=last)
        xs = _ffn(xs, row2(norm_ffn2[i]), w_ffn2_in[i].astype(BF16), w_ffn2_out[i].astype(BF16),
                  row2(norm_final), final=last)
    st = lambda k: jnp.stack(outs[k], axis=0)
    return (xp.reshape(BATCH, SEQ, D_MODEL), xs.reshape(DEC_BATCH, DEC_SEQ, D_MODEL),
            st("diff_p"), st("diff_s"), st("cmp_p"), st("cmp_s"), st("sel_p"), st("sel_s"),
            st("win_p"), st("win_s"))
```

```python
import functools
import math

import jax
import jax.numpy as jnp
from jax import lax
from jax.experimental import pallas as pl
from jax.experimental.pallas import tpu as pltpu

F32 = jnp.float32
BF16 = jnp.bfloat16

D_MODEL = 2048
BATCH = 4
SEQ = 2048
DEPTH = 2
DEC_BATCH = 32
DEC_SEQ = 1
PAST_LEN = 8192
PAGE_SIZE = 128
HEAD_DIM = 128
ROPE_THETA = 10000.0
NORM_EPS = 1e-6
D_FF = 5632
DIFF_HEADS = D_MODEL // (2 * HEAD_DIM)
DIFF_KV_HEADS = 4
DIFF_REP = DIFF_HEADS // DIFF_KV_HEADS
NSA_HEADS = D_MODEL // HEAD_DIM
NSA_GROUPS = 4
NSA_REP = NSA_HEADS // NSA_GROUPS
CMP_LEN = 32
CMP_STRIDE = 16
CMP_HIDDEN = 256
SEL_LEN = 64
SEL_SHIFT = 6
SEL_TOPK = 16
WINDOW = 512
SEL_FORCE = 1e30

N_PAGES = PAST_LEN // PAGE_SIZE
LANES = 128
SUBLANES = 8
VMEM_LIMIT = 56 * 1024 * 1024

TM_DENSE = 1024
TM_FFN = 512
TF_FFN = 512
TN_DENSE = 512
TQ_DIFF = 256
TQ_NSA = 256
DIFF_PAGES_PER_STEP = 8
CMP_ROWS_PER_SLAB = 2048
CMP_CHUNKS_PER_SLAB = CMP_ROWS_PER_SLAB // CMP_STRIDE
CMP_PAGES_PER_SLAB = CMP_ROWS_PER_SLAB // PAGE_SIZE
N_SEL_PAST = PAST_LEN // SEL_LEN
J1_BATCH = 8
SEL_PAST_SLOTS = SEL_TOPK - 1


def _cparams(sem, vmem=VMEM_LIMIT):
    return pltpu.CompilerParams(dimension_semantics=sem, vmem_limit_bytes=vmem)


def _rms(x, g):
    ms = jnp.mean(x * x, axis=-1, keepdims=True)
    return x * lax.rsqrt(ms + NORM_EPS) * g


def _silu(x):
    return x * jax.nn.sigmoid(x)


def _dot(a, b):
    return jnp.dot(a, b, preferred_element_type=F32)


def _dot_nt(a, b):
    return lax.dot_general(a, b, (((1,), (1,)), ((), ())), preferred_element_type=F32)


def _roll_lanes(x, shift):
    n = x.shape[1] // LANES
    parts = [pltpu.roll(x[:, c * LANES:(c + 1) * LANES], shift, 1) for c in range(n)]
    return parts[0] if n == 1 else jnp.concatenate(parts, axis=1)


def _masked_softmax(s, mask):
    s = jnp.where(mask, s, -jnp.inf)
    m = jnp.max(s, axis=-1, keepdims=True)
    m = jnp.where(jnp.isfinite(m), m, 0.0)
    e = jnp.exp(s - m)
    inv = 1.0 / jnp.maximum(jnp.sum(e, axis=-1, keepdims=True), 1e-30)
    return e * inv


def _ffn_kernel(x_ref, g_ref, wg_ref, wu_ref, wo_ref, gf_ref, o_ref, h_ref, *, n_j, final):
    j = pl.program_id(1)

    @pl.when(j == 0)
    def _():
        x = x_ref[...]
        h_ref[...] = _rms(x, g_ref[...]).astype(BF16)
        o_ref[...] = x

    h = h_ref[...]
    gate = _dot(h, wg_ref[...])
    up = _dot(h, wu_ref[...])
    a = (_silu(gate) * up).astype(BF16)
    o_ref[...] += 0.5 * _dot(a, wo_ref[...])

    if final:
        @pl.when(j == n_j - 1)
        def _():
            o_ref[...] = _rms(o_ref[...], gf_ref[...])


def _ffn(x, g, w_in, w_out, g_final, *, final):
    m = x.shape[0]
    tm = min(TM_FFN, m)
    n_j = D_FF // TF_FFN
    return pl.pallas_call(
        functools.partial(_ffn_kernel, n_j=n_j, final=final),
        out_shape=jax.ShapeDtypeStruct((m, D_MODEL), F32),
        grid=(m // tm, n_j),
        in_specs=[
            pl.BlockSpec((tm, D_MODEL), lambda i, j: (i, 0)),
            pl.BlockSpec((1, D_MODEL), lambda i, j: (0, 0)),
            pl.BlockSpec((D_MODEL, TF_FFN), lambda i, j: (0, j)),
            pl.BlockSpec((D_MODEL, TF_FFN), lambda i, j: (0, j + n_j)),
            pl.BlockSpec((TF_FFN, D_MODEL), lambda i, j: (j, 0)),
            pl.BlockSpec((1, D_MODEL), lambda i, j: (0, 0)),
        ],
        out_specs=pl.BlockSpec((tm, D_MODEL), lambda i, j: (i, 0)),
        scratch_shapes=[pltpu.VMEM((tm, D_MODEL), BF16)],
        compiler_params=_cparams(("parallel", "arbitrary")),
        name="ffn",
    )(x, g, w_in, w_in, w_out, g_final)


def _ffn_up_kernel(x_ref, g_ref, wg_ref, wu_ref, o_ref, h_ref):
    @pl.when(pl.program_id(1) == 0)
    def _():
        h_ref[...] = _rms(x_ref[...], g_ref[...]).astype(BF16)

    h = h_ref[...]
    o_ref[...] = (_silu(_dot(h, wg_ref[...])) * _dot(h, wu_ref[...])).astype(BF16)


def _ffn_down_kernel(a_ref, w_ref, x_ref, o_ref):
    o_ref[...] = x_ref[...] + 0.5 * _dot(a_ref[...], w_ref[...])


def _norm_kernel(x_ref, g_ref, o_ref):
    o_ref[...] = _rms(x_ref[...], g_ref[...])


def _ffn_prompt(x, g, w_in, w_out, g_final, *, final):
    m = x.shape[0]
    tm = TM_DENSE
    n_j = D_FF // TF_FFN
    a = pl.pallas_call(
        _ffn_up_kernel,
        out_shape=jax.ShapeDtypeStruct((m, D_FF), BF16),
        grid=(m // tm, n_j),
        in_specs=[
            pl.BlockSpec((tm, D_MODEL), lambda i, j: (i, 0)),
            pl.BlockSpec((1, D_MODEL), lambda i, j: (0, 0)),
            pl.BlockSpec((D_MODEL, TF_FFN), lambda i, j: (0, j)),
            pl.BlockSpec((D_MODEL, TF_FFN), lambda i, j: (0, j + n_j)),
        ],
        out_specs=pl.BlockSpec((tm, TF_FFN), lambda i, j: (i, j)),
        scratch_shapes=[pltpu.VMEM((tm, D_MODEL), BF16)],
        compiler_params=_cparams(("parallel", "arbitrary")),
        name="ffn_up",
    )(x, g, w_in, w_in)
    y = pl.pallas_call(
        _ffn_down_kernel,
        out_shape=jax.ShapeDtypeStruct((m, D_MODEL), F32),
        grid=(m // tm, D_MODEL // TN_DENSE),
        in_specs=[
            pl.BlockSpec((tm, D_FF), lambda i, j: (i, 0)),
            pl.BlockSpec((D_FF, TN_DENSE), lambda i, j: (0, j)),
            pl.BlockSpec((tm, TN_DENSE), lambda i, j: (i, j)),
        ],
        out_specs=pl.BlockSpec((tm, TN_DENSE), lambda i, j: (i, j)),
        compiler_params=_cparams(("parallel", "arbitrary")),
        name="ffn_down",
    )(a, w_out, x)
    if not final:
        return y
    return pl.pallas_call(
        _norm_kernel,
        out_shape=jax.ShapeDtypeStruct((m, D_MODEL), F32),
        grid=(m // tm,),
        in_specs=[pl.BlockSpec((tm, D_MODEL), lambda i: (i, 0)), pl.BlockSpec((1, D_MODEL), lambda i: (0, 0))],
        out_specs=pl.BlockSpec((tm, D_MODEL), lambda i: (i, 0)),
        compiler_params=_cparams(("parallel",)),
        name="final_norm",
    )(y, g_final)


def _rope_cols(y, cos, sin):
    parts = []
    for hh in range(y.shape[1] // HEAD_DIM):
        yh = y[:, hh * HEAD_DIM:(hh + 1) * HEAD_DIM]
        parts.append(yh * cos + pltpu.roll(yh, HEAD_DIM // 2, 1) * sin)
    return parts[0] if len(parts) == 1 else jnp.concatenate(parts, axis=1)


def _proj_kernel(x_ref, g_ref, w_ref, cos_ref, sin_ref, o_ref, h_ref, *, mode, scale):
    j = pl.program_id(1)

    @pl.when(j == 0)
    def _():
        h_ref[...] = _rms(x_ref[...], g_ref[...]).astype(BF16)

    y = _dot(h_ref[...], w_ref[...])
    if mode == "rope":
        o_ref[...] = _rope_cols(y, cos_ref[...], sin_ref[...]) * scale
    elif mode == "rope_even":
        @pl.when(j % 2 == 0)
        def _():
            o_ref[...] = _rope_cols(y, cos_ref[...], sin_ref[...])

        @pl.when(j % 2 == 1)
        def _():
            o_ref[...] = y
    elif mode == "sigmoid":
        o_ref[...] = jax.nn.sigmoid(y)
    else:
        o_ref[...] = y


def _proj(x, g, w, col0, ncols, cos, sin, *, mode, scale=1.0, tn=TN_DENSE):
    m = x.shape[0]
    tm = min(TM_DENSE, m)
    tn = min(tn, ncols)
    jb0 = col0 // tn
    n_pos_tiles = cos.shape[0] // tm
    return pl.pallas_call(
        functools.partial(_proj_kernel, mode=mode, scale=scale),
        out_shape=jax.ShapeDtypeStruct((m, ncols), F32),
        grid=(m // tm, ncols // tn),
        in_specs=[
            pl.BlockSpec((tm, D_MODEL), lambda i, j: (i, 0)),
            pl.BlockSpec((1, D_MODEL), lambda i, j: (0, 0)),
            pl.BlockSpec((D_MODEL, tn), lambda i, j: (0, j + jb0)),
            pl.BlockSpec((tm, HEAD_DIM), lambda i, j: (i % n_pos_tiles, 0)),
            pl.BlockSpec((tm, HEAD_DIM), lambda i, j: (i % n_pos_tiles, 0)),
        ],
        out_specs=pl.BlockSpec((tm, tn), lambda i, j: (i, j)),
        scratch_shapes=[pltpu.VMEM((tm, D_MODEL), BF16)],
        compiler_params=_cparams(("parallel", "arbitrary")),
        name="proj_" + mode,
    )(x, g, w, cos, sin)


def _outproj_kernel(a_ref, w_ref, r_ref, o_ref):
    o_ref[...] = r_ref[...] + _dot(a_ref[...].astype(BF16), w_ref[...])


def _outproj(a, w, res):
    m, k = a.shape
    tm = min(TM_DENSE, m)
    tn = TN_DENSE
    return pl.pallas_call(
        _outproj_kernel,
        out_shape=jax.ShapeDtypeStruct((m, D_MODEL), F32),
        grid=(m // tm, D_MODEL // tn),
        in_specs=[
            pl.BlockSpec((tm, k), lambda i, j: (i, 0)),
            pl.BlockSpec((k, tn), lambda i, j: (0, j)),
            pl.BlockSpec((tm, tn), lambda i, j: (i, j)),
        ],
        out_specs=pl.BlockSpec((tm, tn), lambda i, j: (i, j)),
        compiler_params=_cparams(("parallel", "arbitrary")),
        name="outproj",
    )(a, w, res)


def _diff_lambda(lam, lam_init):
    a = jnp.sum(lam[0:1] * lam[1:2], axis=-1, keepdims=True)
    b = jnp.sum(lam[2:3] * lam[3:4], axis=-1, keepdims=True)
    return jnp.exp(a) - jnp.exp(b) + lam_init


def _subln(o, sub, lam_init):
    return _rms(o, sub) * (1.0 - lam_init)


def _diff_prompt_kernel(q_ref, k_ref, v_ref, lam_ref, sub_ref, o_ref, k1_s, k2_s, v_s, *, lam_init):
    k1_s[...] = k_ref[:, :HEAD_DIM].astype(BF16)
    k2_s[...] = k_ref[:, HEAD_DIM:].astype(BF16)
    v_s[...] = v_ref[...].astype(BF16)
    lmb = _diff_lambda(lam_ref[...], lam_init)
    sub = sub_ref[...]
    tq = TQ_DIFF
    hd2 = 2 * HEAD_DIM
    for qi in range(SEQ // tq):
        q0 = qi * tq
        kl = q0 + tq
        q1 = jnp.concatenate([q_ref[q0:q0 + tq, r * hd2:r * hd2 + HEAD_DIM]
                              for r in range(DIFF_REP)], axis=0).astype(BF16)
        q2 = jnp.concatenate([q_ref[q0:q0 + tq, r * hd2 + HEAD_DIM:(r + 1) * hd2]
                              for r in range(DIFF_REP)], axis=0).astype(BF16)
        rows = DIFF_REP * tq
        pos_q = q0 + (lax.broadcasted_iota(jnp.int32, (rows, kl), 0) & (tq - 1))
        mask = lax.broadcasted_iota(jnp.int32, (rows, kl), 1) <= pos_q
        p1 = _masked_softmax(_dot_nt(q1, k1_s[:kl]), mask)
        p2 = _masked_softmax(_dot_nt(q2, k2_s[:kl]), mask)
        a = (p1 - lmb * p2).astype(BF16)
        o = _subln(_dot(a, v_s[:kl]), sub, lam_init)
        for r in range(DIFF_REP):
            o_ref[q0:q0 + tq, r * hd2:(r + 1) * hd2] = o[r * tq:(r + 1) * tq].astype(BF16)


def _diff_prompt(q, kv, lam, sub, lam_init):
    gw = DIFF_REP * 2 * HEAD_DIM
    return pl.pallas_call(
        functools.partial(_diff_prompt_kernel, lam_init=lam_init),
        out_shape=jax.ShapeDtypeStruct((BATCH * SEQ, D_MODEL), BF16),
        grid=(BATCH, DIFF_KV_HEADS),
        in_specs=[
            pl.BlockSpec((SEQ, gw), lambda b, g: (b, g)),
            pl.BlockSpec((SEQ, 2 * HEAD_DIM), lambda b, g: (b, g)),
            pl.BlockSpec((SEQ, 2 * HEAD_DIM), lambda b, g: (b, DIFF_KV_HEADS + g)),
            pl.BlockSpec((4, HEAD_DIM), lambda b, g: (0, 0)),
            pl.BlockSpec((1, 2 * HEAD_DIM), lambda b, g: (0, 0)),
        ],
        out_specs=pl.BlockSpec((SEQ, gw), lambda b, g: (b, g)),
        scratch_shapes=[pltpu.VMEM((SEQ, HEAD_DIM), BF16), pltpu.VMEM((SEQ, HEAD_DIM), BF16),
                        pltpu.VMEM((SEQ, 2 * HEAD_DIM), BF16)],
        compiler_params=_cparams(("parallel", "parallel")),
        name="diff_prompt",
    )(q, kv, kv, lam, sub)


def _diff_sample_kernel(pt_ref, q_ref, kvn_ref, *rest, lam_init, n_steps):
    del pt_ref
    pages = rest[:DIFF_PAGES_PER_STEP]
    lam_ref, sub_ref, o_ref, m_s, l_s, acc_s = rest[DIFF_PAGES_PER_STEP:]
    step = pl.program_id(1)
    hd2 = 2 * HEAD_DIM
    nq = 2 * DIFF_HEADS
    ncol = PAGE_SIZE * SUBLANES

    @pl.when(step == 0)
    def _():
        m_s[...] = jnp.full(m_s.shape, -jnp.inf, F32)
        l_s[...] = jnp.zeros(l_s.shape, F32)
        acc_s[...] = jnp.zeros(acc_s.shape, F32)

    q = q_ref[...]
    qf = jnp.concatenate([q[:, h * hd2 + m * HEAD_DIM:h * hd2 + (m + 1) * HEAD_DIM]
                          for m in range(2) for h in range(DIFF_HEADS)], axis=0)
    qb = qf.astype(BF16)
    row = lax.broadcasted_iota(jnp.int32, (nq, ncol), 0)
    col = lax.broadcasted_iota(jnp.int32, (nq, ncol), 1)
    own = (((col >> 2) & 1) == (row >> 3)) & ((col & 3) == ((row >> 1) & 3))
    bias = jnp.where(own, 0.0, -jnp.inf)
    s_list = [_dot_nt(qb, pg[:, 0].reshape(ncol, HEAD_DIM).astype(BF16)) + bias for pg in pages]
    m_prev = m_s[:, :1]
    m_new = jnp.maximum(m_prev, jnp.max(functools.reduce(jnp.maximum, s_list), axis=-1, keepdims=True))
    corr = jnp.exp(m_prev - m_new)
    l_new = corr * l_s[:, :1]
    acc = jnp.concatenate([corr, corr], axis=0) * acc_s[...]
    for s, pg in zip(s_list, pages):
        p = jnp.exp(s - m_new)
        l_new = l_new + jnp.sum(p, axis=-1, keepdims=True)
        p1, p2 = p[:DIFF_HEADS], p[DIFF_HEADS:]
        p_lo = jnp.concatenate([p1, _roll_lanes(p2, LANES - DIFF_KV_HEADS)], axis=0)
        p_hi = jnp.concatenate([_roll_lanes(p1, DIFF_KV_HEADS), p2], axis=0)
        lhs = jnp.concatenate([p_lo, p_hi], axis=0).astype(BF16)
        acc = acc + _dot(lhs, pg[:, 1].reshape(ncol, HEAD_DIM).astype(BF16))
    m_s[...] = jnp.broadcast_to(m_new, m_s.shape)
    l_s[...] = jnp.broadcast_to(l_new, l_s.shape)
    acc_s[...] = acc

    @pl.when(step == n_steps - 1)
    def _():
        lmb = _diff_lambda(lam_ref[...], lam_init)
        kvn = kvn_ref[...]
        kv_off = DIFF_KV_HEADS * hd2
        head_rows = lambda f: jnp.concatenate([f(m, h // DIFF_REP) for m in range(2) for h in range(DIFF_HEADS)], 0)
        k_new = head_rows(lambda m, g: kvn[:, g * hd2 + m * HEAD_DIM:g * hd2 + (m + 1) * HEAD_DIM])
        v_lo = head_rows(lambda m, g: kvn[:, kv_off + g * hd2:kv_off + g * hd2 + HEAD_DIM])
        v_hi = head_rows(lambda m, g: kvn[:, kv_off + g * hd2 + HEAD_DIM:kv_off + (g + 1) * hd2])
        s_new = jnp.sum(qf * k_new, axis=-1, keepdims=True)
        m_old = m_s[:, :1]
        m_fin = jnp.maximum(m_old, s_new)
        c_old = jnp.exp(m_old - m_fin)
        p_new = jnp.exp(s_new - m_fin)
        inv = 1.0 / (c_old * l_s[:, :1] + p_new)
        on = jnp.concatenate([(c_old * acc_s[:nq] + p_new * v_lo) * inv,
                              (c_old * acc_s[nq:] + p_new * v_hi) * inv], axis=1)
        o = on[:DIFF_HEADS] - lmb * on[DIFF_HEADS:]
        o_ref[...] = _subln(o, sub_ref[...], lam_init)


def _diff_sample(q, kvn, cache, layer, page_table, lam, sub, lam_init):
    n_steps = N_PAGES // DIFF_PAGES_PER_STEP
    hd2 = 2 * HEAD_DIM
    nq = 2 * DIFF_HEADS
    page_specs = [
        pl.BlockSpec((None, None, PAGE_SIZE, 2, SUBLANES, HEAD_DIM),
                     functools.partial(lambda b, s, pt, k: (layer, pt[b, s * DIFF_PAGES_PER_STEP + k], 0, 0, 0, 0),
                                       k=k))
        for k in range(DIFF_PAGES_PER_STEP)
    ]
    grid_spec = pltpu.PrefetchScalarGridSpec(
        num_scalar_prefetch=1,
        grid=(DEC_BATCH, n_steps),
        in_specs=[
            pl.BlockSpec((None, 1, D_MODEL), lambda b, s, pt: (b, 0, 0)),
            pl.BlockSpec((None, 1, D_MODEL), lambda b, s, pt: (b, 0, 0)),
            *page_specs,
            pl.BlockSpec((4, HEAD_DIM), lambda b, s, pt: (0, 0)),
            pl.BlockSpec((1, hd2), lambda b, s, pt: (0, 0)),
        ],
        out_specs=pl.BlockSpec((None, DIFF_HEADS, hd2), lambda b, s, pt: (b, 0, 0)),
        scratch_shapes=[pltpu.VMEM((nq, LANES), F32), pltpu.VMEM((nq, LANES), F32),
                        pltpu.VMEM((2 * nq, HEAD_DIM), F32)],
    )
    return pl.pallas_call(
        functools.partial(_diff_sample_kernel, lam_init=lam_init, n_steps=n_steps),
        out_shape=jax.ShapeDtypeStruct((DEC_BATCH, DIFF_HEADS, hd2), F32),
        grid_spec=grid_spec,
        compiler_params=_cparams(("parallel", "arbitrary")),
        name="diff_sample",
    )(page_table, q, kvn, *([cache] * DIFF_PAGES_PER_STEP), lam, sub)


def _compress_kernel(*refs, n_src, n_prefetch):
    refs = refs[n_prefetch:]
    srcs = refs[:n_src]
    wcat_ref, pe_ref, b1_ref, w2_ref, o_ref, x_s, carry_s = refs[n_src:]
    step = pl.program_id(1)

    @pl.when(step == 0)
    def _():
        carry_s[...] = jnp.zeros(carry_s.shape, F32)

    planes = 2 * NSA_GROUPS
    nc = CMP_CHUNKS_PER_SLAB
    row0 = lax.broadcasted_iota(jnp.int32, (nc, CMP_HIDDEN), 0) == 0
    rows_per_src = CMP_ROWS_PER_SLAB // n_src
    tiles = [(src, t0) for src in srcs for t0 in range(0, rows_per_src, SUBLANES * CMP_STRIDE)]
    for tp in range(0, len(tiles), 2):
        c0 = (tp // 2) * 2 * SUBLANES
        for r in range(CMP_STRIDE):
            ya, yb = [jnp.swapaxes(src[pl.ds(t0 + r, SUBLANES, stride=CMP_STRIDE)], 0, 1)
                      for src, t0 in tiles[tp:tp + 2]]
            for p in range(planes):
                blk = jnp.concatenate([ya[p], yb[p]], axis=0).astype(BF16)
                g = p % NSA_GROUPS
                x_s[p // NSA_GROUPS, g * nc + c0:g * nc + c0 + 2 * SUBLANES, r * HEAD_DIM:(r + 1) * HEAD_DIM] = blk
    for n in range(2):
        wcat = wcat_ref[n]
        fs = _dot(x_s[n], wcat)
        pb = _dot(pe_ref[n].astype(BF16), wcat)
        bias = pb[0:1, :CMP_HIDDEN] + pb[1:2, CMP_HIDDEN:] + b1_ref[n]
        for g in range(NSA_GROUPS):
            first = fs[g * nc:(g + 1) * nc, :CMP_HIDDEN]
            second = fs[g * nc:(g + 1) * nc, CMP_HIDDEN:]
            prev_first = jnp.where(row0, carry_s[n, g][0:1], pltpu.roll(first, 1, 0))
            carry_s[n, g] = jnp.broadcast_to(first[nc - 1:nc], (SUBLANES, CMP_HIDDEN))
            hid = _silu(prev_first + second + bias)
            o_ref[n, g] = _dot(hid.astype(BF16), w2_ref[n])


def _compress_specs(idx):
    return [
        pl.BlockSpec((2, CMP_STRIDE * HEAD_DIM, 2 * CMP_HIDDEN), idx(lambda: (0, 0, 0))),
        pl.BlockSpec((2, SUBLANES, CMP_STRIDE * HEAD_DIM), idx(lambda: (0, 0, 0))),
        pl.BlockSpec((2, 1, CMP_HIDDEN), idx(lambda: (0, 0, 0))),
        pl.BlockSpec((2, CMP_HIDDEN, HEAD_DIM), idx(lambda: (0, 0, 0))),
    ]


_CMP_SCRATCH = [pltpu.VMEM((2, NSA_GROUPS * CMP_CHUNKS_PER_SLAB, CMP_STRIDE * HEAD_DIM), BF16),
                pltpu.VMEM((2, NSA_GROUPS, SUBLANES, CMP_HIDDEN), F32)]


def _compress_prompt(rows, cw):
    n_slab = SEQ // CMP_ROWS_PER_SLAB
    planes = 2 * NSA_GROUPS
    return pl.pallas_call(
        functools.partial(_compress_kernel, n_src=1, n_prefetch=0),
        out_shape=jax.ShapeDtypeStruct((BATCH, 2, NSA_GROUPS, n_slab * CMP_CHUNKS_PER_SLAB, HEAD_DIM), F32),
        grid=(BATCH, n_slab),
        in_specs=[pl.BlockSpec((CMP_ROWS_PER_SLAB, planes, HEAD_DIM), lambda b, s: (b * n_slab + s, 0, 0))]
        + _compress_specs(lambda f: (lambda b, s: f())),
        out_specs=pl.BlockSpec((None, 2, NSA_GROUPS, CMP_CHUNKS_PER_SLAB, HEAD_DIM), lambda b, s: (b, 0, 0, s, 0)),
        scratch_shapes=_CMP_SCRATCH,
        compiler_params=_cparams(("parallel", "arbitrary")),
        name="compress_prompt",
    )(rows, *cw)


def _compress_sample(cache, layer, page_table, cw):
    n_slab = PAST_LEN // CMP_ROWS_PER_SLAB
    planes = 2 * NSA_GROUPS
    page_specs = [
        pl.BlockSpec((None, None, PAGE_SIZE, planes, HEAD_DIM),
                     functools.partial(lambda b, s, pt, k: (layer, pt[b, s * CMP_PAGES_PER_SLAB + k], 0, 0, 0), k=k))
        for k in range(CMP_PAGES_PER_SLAB)
    ]
    grid_spec = pltpu.PrefetchScalarGridSpec(
        num_scalar_prefetch=1,
        grid=(DEC_BATCH, n_slab),
        in_specs=page_specs + _compress_specs(lambda f: (lambda b, s, pt: f())),
        out_specs=pl.BlockSpec((None, 2, NSA_GROUPS, CMP_CHUNKS_PER_SLAB, HEAD_DIM),
                               lambda b, s, pt: (b, 0, 0, s, 0)),
        scratch_shapes=_CMP_SCRATCH,
    )
    return pl.pallas_call(
        functools.partial(_compress_kernel, n_src=CMP_PAGES_PER_SLAB, n_prefetch=1),
        out_shape=jax.ShapeDtypeStruct((DEC_BATCH, 2, NSA_GROUPS, n_slab * CMP_CHUNKS_PER_SLAB, HEAD_DIM), F32),
        grid_spec=grid_spec,
        compiler_params=_cparams(("parallel", "arbitrary")),
        name="compress_sample",
    )(page_table, *([cache] * CMP_PAGES_PER_SLAB), *cw)


def _topk_rank(score, blk, n_blocks):
    rank = jnp.zeros(score.shape, F32)
    for sp in range(n_blocks):
        c = score[:, sp:sp + 1]
        before = (c > score) | ((c == score) & (blk > sp))
        rank = rank + jnp.where(before, 1.0, 0.0)
    return rank


def _nsa_prompt_kernel(q_ref, kvc_ref, ks_ref, vs_ref, kw_ref, vw_ref, gt_ref, cover_ref, expand_ref,
                       o_ref, ks_s, vs_s, kw_s, vw_s):
    ks_s[...] = ks_ref[...].astype(BF16)
    vs_s[...] = vs_ref[...].astype(BF16)
    kw_s[...] = kw_ref[...].astype(BF16)
    vw_s[...] = vw_ref[...].astype(BF16)
    kc = kvc_ref[0].astype(BF16)
    vc = kvc_ref[1].astype(BF16)
    n_ent = kc.shape[0]
    tq = TQ_NSA
    rows = NSA_REP * tq
    for qi in range(SEQ // tq):
        q0 = qi * tq
        kl = q0 + tq
        qb = jnp.concatenate([q_ref[q0:q0 + tq, r * HEAD_DIM:(r + 1) * HEAD_DIM]
                              for r in range(NSA_REP)], axis=0).astype(BF16)

        pos_c = q0 + (lax.broadcasted_iota(jnp.int32, (rows, n_ent), 0) & (tq - 1))
        ent = lax.broadcasted_iota(jnp.int32, (rows, n_ent), 1)
        mask_c = (ent >= 1) & (ent * CMP_STRIDE + (CMP_LEN - 1 - CMP_STRIDE) <= pos_c)
        p_c = _masked_softmax(_dot_nt(qb, kc), mask_c)
        o_c = _dot(p_c.astype(BF16), vc)

        psum = functools.reduce(lambda a, b: a + b, [p_c[r * tq:(r + 1) * tq] for r in range(NSA_REP)])
        imp = _dot(psum.astype(BF16), cover_ref[...])
        blk = lax.broadcasted_iota(jnp.int32, (tq, LANES), 1)
        cur = (q0 + lax.broadcasted_iota(jnp.int32, (tq, LANES), 0)) >> SEL_SHIFT
        n_blk = kl // SEL_LEN
        if n_blk <= SEL_TOPK:
            sel = jnp.where(blk <= cur, 1.0, 0.0)
        else:
            forced = (blk == 0) | (blk == cur) | (blk == cur - 1)
            score = jnp.where(forced, SEL_FORCE, jnp.where(blk <= cur, imp, -SEL_FORCE))
            rank = _topk_rank(score, blk, n_blk)
            sel = jnp.where((rank < SEL_TOPK) & (blk <= cur), 1.0, 0.0)
        selk = _dot(sel.astype(BF16), expand_ref[:, :kl])
        selk = jnp.concatenate([selk] * NSA_REP, axis=0)
        pos_s = q0 + (lax.broadcasted_iota(jnp.int32, (rows, kl), 0) & (tq - 1))
        key_s = lax.broadcasted_iota(jnp.int32, (rows, kl), 1)
        mask_s = (selk > 0.5) & (key_s <= pos_s)
        p_s = _masked_softmax(_dot_nt(qb, ks_s[:kl]), mask_s)
        o_s = _dot(p_s.astype(BF16), vs_s[:kl])

        w0 = max(0, q0 - WINDOW)
        wl = kl - w0
        rel = (q0 - w0) + (lax.broadcasted_iota(jnp.int32, (rows, wl), 0) & (tq - 1)) \
            - lax.broadcasted_iota(jnp.int32, (rows, wl), 1)
        mask_w = (rel >= 0) & (rel < WINDOW)
        p_w = _masked_softmax(_dot_nt(qb, kw_s[w0:kl]), mask_w)
        o_w = _dot(p_w.astype(BF16), vw_s[w0:kl])

        gt = gt_ref[q0:q0 + tq, :]
        for r in range(NSA_REP):
            sl = slice(r * tq, (r + 1) * tq)
            o = (gt[:, 3 * r:3 * r + 1] * o_c[sl] + gt[:, 3 * r + 1:3 * r + 2] * o_s[sl]
                 + gt[:, 3 * r + 2:3 * r + 3] * o_w[sl])
            o_ref[q0:q0 + tq, r * HEAD_DIM:(r + 1) * HEAD_DIM] = o.astype(BF16)


def _nsa_prompt(q, kvc, sel_rows, win_rows, gates, cover, expand):
    gw = NSA_REP * HEAD_DIM
    n_ent = kvc.shape[3]
    row_spec_k = pl.BlockSpec((SEQ, HEAD_DIM), lambda b, g: (b, g))
    row_spec_v = pl.BlockSpec((SEQ, HEAD_DIM), lambda b, g: (b, NSA_GROUPS + g))
    return pl.pallas_call(
        _nsa_prompt_kernel,
        out_shape=jax.ShapeDtypeStruct((BATCH * SEQ, D_MODEL), BF16),
        grid=(BATCH, NSA_GROUPS),
        in_specs=[
            pl.BlockSpec((SEQ, gw), lambda b, g: (b, g)),
            pl.BlockSpec((None, 2, None, n_ent, HEAD_DIM), lambda b, g: (b, 0, g, 0, 0)),
            row_spec_k, row_spec_v, row_spec_k, row_spec_v,
            pl.BlockSpec((SEQ, LANES), lambda b, g: (b, g)),
            pl.BlockSpec(cover.shape, lambda b, g: (0, 0)),
            pl.BlockSpec(expand.shape, lambda b, g: (0, 0)),
        ],
        out_specs=pl.BlockSpec((SEQ, gw), lambda b, g: (b, g)),
        scratch_shapes=[pltpu.VMEM((SEQ, HEAD_DIM), BF16)] * 4,
        compiler_params=_cparams(("parallel", "parallel")),
        name="nsa_prompt",
    )(q, kvc, sel_rows, sel_rows, win_rows, win_rows, gates, cover, expand)


def _head_rows(q):
    parts = [q[:, r * HEAD_DIM:(r + 1) * HEAD_DIM] for r in range(NSA_REP)]
    parts.append(jnp.zeros((SUBLANES - NSA_REP, HEAD_DIM), F32))
    return jnp.concatenate(parts, axis=0)


def _nsa_sample_cmp_kernel(q_ref, kvc_ref, cover_ref, o_ref, idx_ref):
    n_ent = kvc_ref.shape[2]
    ent = lax.broadcasted_iota(jnp.int32, (SUBLANES, n_ent), 1)
    mask_c = ent >= 1
    psums = []
    for bi in range(J1_BATCH):
        qh = _head_rows(q_ref[bi:bi + 1, :])
        p = _masked_softmax(_dot_nt(qh.astype(BF16), kvc_ref[bi, 0].astype(BF16)), mask_c)
        o = _dot(p.astype(BF16), kvc_ref[bi, 1].astype(BF16))
        for r in range(NSA_REP):
            o_ref[bi:bi + 1, r * HEAD_DIM:(r + 1) * HEAD_DIM] = o[r:r + 1]
        psums.append(functools.reduce(lambda a, b: a + b, [p[r:r + 1] for r in range(NSA_REP)]))
    psum = jnp.concatenate(psums, axis=0)
    imp = _dot(psum.astype(BF16), cover_ref[...])
    blk = lax.broadcasted_iota(jnp.int32, imp.shape, 1)
    cur = N_SEL_PAST
    forced = (blk == 0) | (blk == cur) | (blk == cur - 1)
    score = jnp.where(forced, SEL_FORCE, jnp.where(blk <= cur, imp, -SEL_FORCE))
    rank = _topk_rank(score, blk, cur + 1)
    blk_f = blk.astype(F32)
    lane = lax.broadcasted_iota(jnp.int32, (J1_BATCH, LANES), 1)
    out = jnp.zeros((J1_BATCH, LANES), F32)
    for k in range(SEL_TOPK):
        ik = jnp.sum(jnp.where(rank == float(k), blk_f, 0.0), axis=-1, keepdims=True)
        out = out + jnp.where(lane == k, ik, 0.0)
    idx_ref[...] = out.astype(jnp.int32)


def _nsa_sample_cmp(q, kvc, cover):
    gw = NSA_REP * HEAD_DIM
    n_ent = kvc.shape[3]
    return pl.pallas_call(
        _nsa_sample_cmp_kernel,
        out_shape=(jax.ShapeDtypeStruct((DEC_BATCH, D_MODEL), F32),
                   jax.ShapeDtypeStruct((NSA_GROUPS, DEC_BATCH, LANES), jnp.int32)),
        grid=(NSA_GROUPS, DEC_BATCH // J1_BATCH),
        in_specs=[
            pl.BlockSpec((J1_BATCH, gw), lambda g, c: (c, g)),
            pl.BlockSpec((J1_BATCH, 2, None, n_ent, HEAD_DIM), lambda g, c: (c, 0, g, 0, 0)),
            pl.BlockSpec(cover.shape, lambda g, c: (0, 0)),
        ],
        out_specs=(pl.BlockSpec((J1_BATCH, gw), lambda g, c: (c, g)),
                   pl.BlockSpec((None, J1_BATCH, LANES), lambda g, c: (g, c, 0))),
        compiler_params=_cparams(("parallel", "parallel")),
        name="nsa_sample_cmp",
    )(q, kvc, cover)


def _nsa_sample_kernel(pt_ref, idx_ref, q_ref, oc_ref, gt_ref, ksn_ref, vsn_ref, kwn_ref, vwn_ref,
                       win_ref, *rest):
    del pt_ref, idx_ref
    blocks = rest[:SEL_PAST_SLOTS]
    o_ref = rest[SEL_PAST_SLOTS]
    g = pl.program_id(1)
    planes = 2 * NSA_GROUPS
    qh = _head_rows(q_ref[...])
    qb = qh.astype(BF16)

    def own_bias(n_rows, first_token):
        col = lax.broadcasted_iota(jnp.int32, (SUBLANES, n_rows), 1)
        return jnp.where(((col & (planes - 1)) == g) & (col >= first_token * planes), 0.0, -jnp.inf)

    to_values = lambda e: _roll_lanes(e, NSA_GROUPS).astype(BF16)

    bias_s = own_bias(SEL_LEN * planes, 0)
    kv_list = [blk[...].astype(BF16) for blk in blocks]
    s_list = [_dot_nt(qb, kv) + bias_s for kv in kv_list]
    s_new = jnp.sum(qh * ksn_ref[...], axis=-1, keepdims=True)
    m = jnp.maximum(jnp.max(functools.reduce(jnp.maximum, s_list), axis=-1, keepdims=True), s_new)
    e_new = jnp.exp(s_new - m)
    den = e_new
    acc = e_new * vsn_ref[...]
    for s, kv in zip(s_list, kv_list):
        e = jnp.exp(s - m)
        den = den + jnp.sum(e, axis=-1, keepdims=True)
        acc = acc + _dot(to_values(e), kv)
    o_s = acc * (1.0 / jnp.maximum(den, 1e-30))

    win = win_ref[...].astype(BF16)
    s_w = _dot_nt(qb, win) + own_bias(win.shape[0], 1)
    s_wn = jnp.sum(qh * kwn_ref[...], axis=-1, keepdims=True)
    m_w = jnp.maximum(jnp.max(s_w, axis=-1, keepdims=True), s_wn)
    e_w = jnp.exp(s_w - m_w)
    e_wn = jnp.exp(s_wn - m_w)
    den_w = jnp.sum(e_w, axis=-1, keepdims=True) + e_wn
    o_w = (_dot(to_values(e_w), win) + e_wn * vwn_ref[...]) * (1.0 / jnp.maximum(den_w, 1e-30))

    gt = gt_ref[...]
    for r in range(NSA_REP):
        o = (gt[:, 3 * r:3 * r + 1] * oc_ref[:, r * HEAD_DIM:(r + 1) * HEAD_DIM]
             + gt[:, 3 * r + 1:3 * r + 2] * o_s[r:r + 1] + gt[:, 3 * r + 2:3 * r + 3] * o_w[r:r + 1])
        o_ref[:, r * HEAD_DIM:(r + 1) * HEAD_DIM] = o


def _nsa_sample(q, o_c, gates, sel_new, win_new, win_state, cache_sel, layer, page_table, idx):
    gw = NSA_REP * HEAD_DIM
    planes = 2 * NSA_GROUPS
    sub = PAGE_SIZE // SEL_LEN
    slots = [0, 1] + list(range(3, SEL_TOPK))

    def half_page(b, g, pt, ix, slot):
        blk = jnp.minimum(ix[b, g * SEL_TOPK + slot], N_SEL_PAST - 1)
        return pt[b, blk // sub] * sub + blk % sub

    def block_spec(slot):
        return pl.BlockSpec((None, None, SEL_LEN * planes, HEAD_DIM),
                            lambda b, g, pt, ix: (layer, half_page(b, g, pt, ix, slot), 0, 0))

    row = lambda voff: pl.BlockSpec((None, 1, HEAD_DIM), lambda b, g, pt, ix: (b, 0, voff + g))
    grid_spec = pltpu.PrefetchScalarGridSpec(
        num_scalar_prefetch=2,
        grid=(DEC_BATCH, NSA_GROUPS),
        in_specs=[
            pl.BlockSpec((None, 1, gw), lambda b, g, pt, ix: (b, 0, g)),
            pl.BlockSpec((None, 1, gw), lambda b, g, pt, ix: (b, 0, g)),
            pl.BlockSpec((None, 1, LANES), lambda b, g, pt, ix: (b, 0, g)),
            row(0), row(NSA_GROUPS), row(0), row(NSA_GROUPS),
            pl.BlockSpec((None, None, win_state.shape[2], HEAD_DIM), lambda b, g, pt, ix: (layer, b, 0, 0)),
            *[block_spec(s) for s in slots],
        ],
        out_specs=pl.BlockSpec((None, 1, gw), lambda b, g, pt, ix: (b, 0, g)),
    )
    return pl.pallas_call(
        _nsa_sample_kernel,
        out_shape=jax.ShapeDtypeStruct((DEC_BATCH, 1, D_MODEL), F32),
        grid_spec=grid_spec,
        compiler_params=_cparams(("parallel", "arbitrary")),
        name="nsa_sample",
    )(page_table, idx, q, o_c, gates, sel_new, sel_new, win_new, win_new, win_state,
      *([cache_sel] * SEL_PAST_SLOTS))


def _rope_tables(pos):
    half = HEAD_DIM // 2
    inv = ROPE_THETA ** (-jnp.arange(half, dtype=F32) / half)
    ang = pos.astype(F32)[:, None] * inv[None, :]
    cos, sin = jnp.cos(ang), jnp.sin(ang)
    return jnp.concatenate([cos, cos], axis=-1), jnp.concatenate([-sin, sin], axis=-1)


def _cover_table(n_ent, n_sel, n_lanes):
    start = (jnp.arange(n_ent)[:, None] - 1) * CMP_STRIDE
    bstart = jnp.arange(n_lanes)[None, :] * SEL_LEN
    hit = (start < bstart + SEL_LEN) & (start + CMP_LEN > bstart)
    hit = hit & (jnp.arange(n_ent)[:, None] >= 1) & (jnp.arange(n_lanes)[None, :] < n_sel)
    return hit.astype(BF16)


def _expand_table(n_keys):
    return (jnp.arange(LANES)[:, None] == (jnp.arange(n_keys)[None, :] // SEL_LEN)).astype(BF16)


def kernel(x_prompt, x_sample, cache_diff, cache_nsa_cmp, cache_nsa_sel, state_nsa_win, page_table,
           norm_ffn1, w_ffn1_in, w_ffn1_out, norm_mix, norm_ffn2, w_ffn2_in, w_ffn2_out,
           w_diff_qkv, diff_lambda, diff_subln, w_diff_o,
           w_nsa_in, nsa_cmp_pe, nsa_cmp_w1, nsa_cmp_b1, nsa_cmp_w2, w_nsa_o, norm_final):
    xp = x_prompt.reshape(BATCH * SEQ, D_MODEL)
    xs = x_sample.reshape(DEC_BATCH * DEC_SEQ, D_MODEL)
    n_pool = cache_diff.shape[1]
    wb = state_nsa_win.shape[2]
    gw2 = 2 * NSA_GROUPS * HEAD_DIM
    cos_p, sin_p = _rope_tables(jnp.arange(SEQ))
    cos_s, sin_s = _rope_tables(jnp.full((DEC_BATCH,), PAST_LEN))
    qscale = HEAD_DIM ** -0.5
    row2 = lambda v: v.reshape(1, -1)

    outs = {k: [] for k in ("diff_p", "diff_s", "cmp_p", "cmp_s", "sel_p", "sel_s", "win_p", "win_s")}
    for i in range(DEPTH):
        last = i == DEPTH - 1
        layer = i // 2
        xp = _ffn_prompt(xp, row2(norm_ffn1[i]), w_ffn1_in[i].astype(BF16), w_ffn1_out[i].astype(BF16),
                  row2(norm_final), final=False)
        xs = _ffn(xs, row2(norm_ffn1[i]), w_ffn1_in[i].astype(BF16), w_ffn1_out[i].astype(BF16),
                  row2(norm_final), final=False)
        gmix = row2(norm_mix[i])
        if i % 2 == 0:
            lam_init = 0.8 - 0.6 * math.exp(-0.3 * i)
            w = w_diff_qkv[layer].astype(BF16)
            qd = 2 * DIFF_HEADS * HEAD_DIM
            kd = 2 * DIFF_KV_HEADS * HEAD_DIM
            lam = diff_lambda[layer]
            sub = row2(diff_subln[layer])
            wo = w_diff_o[layer].astype(BF16)
            res = []
            for x, cos, sin in ((xp, cos_p, sin_p), (xs, cos_s, sin_s)):
                q = _proj(x, gmix, w, 0, qd, cos, sin, mode="rope", scale=qscale)
                kv = _proj(x, gmix, w, qd, 2 * kd, cos, sin, mode="rope_even", tn=kd)
                res.append((q, kv))
            (qp, kvp), (qs, kvs) = res
            op = _diff_prompt(qp, kvp, lam, sub, lam_init)
            cache = cache_diff.reshape(cache_diff.shape[0], n_pool, PAGE_SIZE, 2, DIFF_KV_HEADS, 2, HEAD_DIM)
            cache = jnp.transpose(cache, (0, 1, 2, 3, 5, 4, 6)).reshape(
                cache_diff.shape[0], n_pool, PAGE_SIZE, 2, 2 * DIFF_KV_HEADS, HEAD_DIM)
            os_ = _diff_sample(qs.reshape(DEC_BATCH, 1, D_MODEL), kvs.reshape(DEC_BATCH, 1, D_MODEL), cache, layer,
                               page_table, lam, sub, lam_init).reshape(DEC_BATCH, D_MODEL)
            xp = _outproj(op, wo, xp)
            xs = _outproj(os_, wo, xs)
            outs["diff_p"].append(kvp.reshape(BATCH, SEQ, 2, DIFF_KV_HEADS, 2 * HEAD_DIM))
            outs["diff_s"].append(kvs.reshape(DEC_BATCH, DEC_SEQ, 2, DIFF_KV_HEADS, 2 * HEAD_DIM))
        else:
            w_in = w_nsa_in[layer]
            w = w_in.astype(BF16)
            qd = NSA_HEADS * HEAD_DIM
            wg = w_in[:, qd + 3 * gw2:].reshape(D_MODEL, NSA_GROUPS, NSA_REP * 3)
            wg = jnp.pad(wg, ((0, 0), (0, 0), (0, LANES - NSA_REP * 3))).reshape(D_MODEL, NSA_GROUPS * LANES)
            wg = wg.astype(BF16)
            wo = w_nsa_o[layer].astype(BF16)
            w1 = nsa_cmp_w1[layer]
            wcat = jnp.concatenate([w1[:, :CMP_STRIDE].reshape(2, CMP_STRIDE * HEAD_DIM, CMP_HIDDEN),
                                    w1[:, CMP_STRIDE:].reshape(2, CMP_STRIDE * HEAD_DIM, CMP_HIDDEN)],
                                   axis=-1).astype(BF16)
            pe = nsa_cmp_pe[layer].reshape(2, 2, CMP_STRIDE * HEAD_DIM)
            pe = jnp.pad(pe, ((0, 0), (0, SUBLANES - 2), (0, 0)))
            cw = (wcat, pe, nsa_cmp_b1[layer].reshape(2, 1, CMP_HIDDEN), nsa_cmp_w2[layer].astype(BF16))
            res = []
            for x, cos, sin in ((xp, cos_p, sin_p), (xs, cos_s, sin_s)):
                q = _proj(x, gmix, w, 0, qd, cos, sin, mode="rope", scale=qscale)
                rows = [_proj(x, gmix, w, qd + c * gw2, gw2, cos, sin, mode="rope_even") for c in range(3)]
                gates = _proj(x, gmix, wg, 0, NSA_GROUPS * LANES, cos, sin, mode="sigmoid")
                res.append((q, rows, gates))
            (qp, rows_p, gates_p), (qs, rows_s, gates_s) = res

            kvc_p = _compress_prompt(rows_p[0].reshape(BATCH * SEQ, 2 * NSA_GROUPS, HEAD_DIM), cw)
            n_ent_p = kvc_p.shape[3]
            op = _nsa_prompt(qp, kvc_p, rows_p[1], rows_p[2], gates_p,
                             _cover_table(n_ent_p, SEQ // SEL_LEN, LANES), _expand_table(SEQ))

            cache_c = cache_nsa_cmp.reshape(cache_nsa_cmp.shape[0], n_pool, PAGE_SIZE, 2 * NSA_GROUPS, HEAD_DIM)
            kvc_s = _compress_sample(cache_c, layer, page_table, cw)
            o_c, idx = _nsa_sample_cmp(qs, kvc_s, _cover_table(kvc_s.shape[3], N_SEL_PAST + 1, 2 * LANES))
            idx = jnp.transpose(idx[:, :, :SEL_TOPK], (1, 0, 2)).reshape(DEC_BATCH, NSA_GROUPS * SEL_TOPK)
            planes = 2 * NSA_GROUPS
            cache_s = cache_nsa_sel.reshape(cache_nsa_sel.shape[0], n_pool * (PAGE_SIZE // SEL_LEN),
                                            SEL_LEN * planes, HEAD_DIM)
            win_state = state_nsa_win.reshape(state_nsa_win.shape[0], DEC_BATCH, wb * planes, HEAD_DIM)
            r3 = lambda a: a.reshape(DEC_BATCH, 1, a.shape[-1])
            os_ = _nsa_sample(r3(qs), r3(o_c), r3(gates_s), r3(rows_s[1]), r3(rows_s[2]), win_state, cache_s,
                              layer, page_table, idx).reshape(DEC_BATCH, D_MODEL)
            xp = _outproj(op, wo, xp)
            xs = _outproj(os_, wo, xs)
            shp = (2, NSA_GROUPS, HEAD_DIM)
            outs["cmp_p"].append(rows_p[0].reshape(BATCH, SEQ, *shp))
            outs["cmp_s"].append(rows_s[0].reshape(DEC_BATCH, DEC_SEQ, *shp))
            outs["sel_p"].append(rows_p[1].reshape(BATCH, SEQ, *shp))
            outs["sel_s"].append(rows_s[1].reshape(DEC_BATCH, DEC_SEQ, *shp))
            wp = rows_p[2].reshape(BATCH, SEQ, *shp)
            outs["win_p"].append(wp[:, SEQ - min(WINDOW, SEQ):])
            kvw = jnp.concatenate([state_nsa_win[layer], rows_s[2].reshape(DEC_BATCH, DEC_SEQ, *shp)], axis=1)
            outs["win_s"].append(kvw[:, DEC_SEQ:])
        xp = _ffn_prompt(xp, row2(norm_ffn2[i]), w_ffn2_in[i].astype(BF16), w_ffn2_out[i].astype(BF16),
                  row2(norm_final), final=last)
        xs = _ffn(xs, row2(norm_ffn2[i]), w_ffn2_in[i].astype(BF16), w_ffn2_out[i].astype(BF16),
                  row2(norm_final), final=last)
    st = lambda k: jnp.stack(outs[k], axis=0)
    return (xp.reshape(BATCH, SEQ, D_MODEL), xs.reshape(DEC_BATCH, DEC_SEQ, D_MODEL),
            st("diff_p"), st("diff_s"), st("cmp_p"), st("cmp_s"), st("sel_p"), st("sel_s"),
            st("win_p"), st("win_s"))
```

```python
import functools
import math

import jax
import jax.numpy as jnp
from jax import lax
from jax.experimental import pallas as pl
from jax.experimental.pallas import tpu as pltpu

F32 = jnp.float32
BF16 = jnp.bfloat16

D_MODEL = 2048
BATCH = 4
SEQ = 2048
DEPTH = 2
DEC_BATCH = 32
DEC_SEQ = 1
PAST_LEN = 8192
PAGE_SIZE = 128
HEAD_DIM = 128
ROPE_THETA = 10000.0
NORM_EPS = 1e-6
D_FF = 5632
DIFF_HEADS = D_MODEL // (2 * HEAD_DIM)
DIFF_KV_HEADS = 4
DIFF_REP = DIFF_HEADS // DIFF_KV_HEADS
NSA_HEADS = D_MODEL // HEAD_DIM
NSA_GROUPS = 4
NSA_REP = NSA_HEADS // NSA_GROUPS
CMP_LEN = 32
CMP_STRIDE = 16
CMP_HIDDEN = 256
SEL_LEN = 64
SEL_SHIFT = 6
SEL_TOPK = 16
WINDOW = 512
SEL_FORCE = 1e30

N_PAGES = PAST_LEN // PAGE_SIZE
LANES = 128
SUBLANES = 8
VMEM_LIMIT = 56 * 1024 * 1024

TM_DENSE = 1024
TM_FFN = 512
TF_FFN = 512
TN_DENSE = 512
ROW_CHUNK = 256
TQ_DIFF = 256
TQ_NSA = 256
DIFF_PAGES_PER_STEP = 8
CMP_ROWS_PER_SLAB = 2048
CMP_CHUNKS_PER_SLAB = CMP_ROWS_PER_SLAB // CMP_STRIDE
CMP_PAGES_PER_SLAB = CMP_ROWS_PER_SLAB // PAGE_SIZE
CMP_PARTS = 2
N_SEL_PAST = PAST_LEN // SEL_LEN
J1_BATCH = 8
SEL_PAST_SLOTS = SEL_TOPK - 1


def _cparams(sem, vmem=VMEM_LIMIT):
    return pltpu.CompilerParams(dimension_semantics=sem, vmem_limit_bytes=vmem)


def _rms(x, g):
    ms = jnp.mean(x * x, axis=-1, keepdims=True)
    return x * lax.rsqrt(ms + NORM_EPS) * g


def _silu(x):
    return x * jax.nn.sigmoid(x)


def _dot(a, b):
    return jnp.dot(a, b, preferred_element_type=F32)


def _dot_nt(a, b):
    return lax.dot_general(a, b, (((1,), (1,)), ((), ())), preferred_element_type=F32)


def _roll_lanes(x, shift):
    n = x.shape[1] // LANES
    parts = [pltpu.roll(x[:, c * LANES:(c + 1) * LANES], shift, 1) for c in range(n)]
    return parts[0] if n == 1 else jnp.concatenate(parts, axis=1)


def _head_bias(mask):
    bias = jnp.where(mask, 0.0, -jnp.inf)
    return jnp.concatenate([bias] * NSA_REP, axis=0)


def _biased_softmax(s, bias):
    s = s + bias
    m = jnp.max(s, axis=-1, keepdims=True)
    m = jnp.where(jnp.isfinite(m), m, 0.0)
    e = jnp.exp(s - m)
    inv = 1.0 / jnp.maximum(jnp.sum(e, axis=-1, keepdims=True), 1e-30)
    return e * inv


def _masked_softmax(s, mask):
    s = jnp.where(mask, s, -jnp.inf)
    m = jnp.max(s, axis=-1, keepdims=True)
    m = jnp.where(jnp.isfinite(m), m, 0.0)
    e = jnp.exp(s - m)
    inv = 1.0 / jnp.maximum(jnp.sum(e, axis=-1, keepdims=True), 1e-30)
    return e * inv


def _ffn_kernel(x_ref, g_ref, wg_ref, wu_ref, wo_ref, gf_ref, o_ref, h_ref, *, n_j, final):
    j = pl.program_id(1)

    @pl.when(j == 0)
    def _():
        x = x_ref[...]
        h_ref[...] = _rms(x, g_ref[...]).astype(BF16)
        o_ref[...] = x

    h = h_ref[...]
    gate = _dot(h, wg_ref[...].astype(BF16))
    up = _dot(h, wu_ref[...].astype(BF16))
    a = (_silu(gate) * up).astype(BF16)
    o_ref[...] += 0.5 * _dot(a, wo_ref[...])

    if final:
        @pl.when(j == n_j - 1)
        def _():
            o_ref[...] = _rms(o_ref[...], gf_ref[...])


def _ffn(x, g, w_in, layer, w_out, g_final, *, final):
    m = x.shape[0]
    tm = min(TM_FFN, m)
    n_j = D_FF // TF_FFN
    return pl.pallas_call(
        functools.partial(_ffn_kernel, n_j=n_j, final=final),
        out_shape=jax.ShapeDtypeStruct((m, D_MODEL), F32),
        grid=(m // tm, n_j),
        in_specs=[
            pl.BlockSpec((tm, D_MODEL), lambda i, j: (i, 0)),
            pl.BlockSpec((1, D_MODEL), lambda i, j: (0, 0)),
            pl.BlockSpec((None, D_MODEL, TF_FFN), lambda i, j: (layer, 0, j)),
            pl.BlockSpec((None, D_MODEL, TF_FFN), lambda i, j: (layer, 0, j + n_j)),
            pl.BlockSpec((TF_FFN, D_MODEL), lambda i, j: (j, 0)),
            pl.BlockSpec((1, D_MODEL), lambda i, j: (0, 0)),
        ],
        out_specs=pl.BlockSpec((tm, D_MODEL), lambda i, j: (i, 0)),
        scratch_shapes=[pltpu.VMEM((tm, D_MODEL), BF16)],
        compiler_params=_cparams(("parallel", "arbitrary")),
        name="ffn",
    )(x, g, w_in, w_in, w_out, g_final)


def _ffn_up_kernel(x_ref, g_ref, wg_ref, wu_ref, o_ref, h_ref):
    @pl.when(pl.program_id(1) == 0)
    def _():
        h_ref[...] = _rms(x_ref[...], g_ref[...]).astype(BF16)

    h = h_ref[...]
    o_ref[...] = (_silu(_dot(h, wg_ref[...].astype(BF16))) * _dot(h, wu_ref[...].astype(BF16))).astype(BF16)


def _ffn_down_kernel(a_ref, w_ref, x_ref, o_ref):
    o_ref[...] = x_ref[...] + 0.5 * _dot(a_ref[...], w_ref[...])


def _norm_kernel(x_ref, g_ref, o_ref):
    o_ref[...] = _rms(x_ref[...], g_ref[...])


def _ffn_prompt(x, g, w_in, layer, w_out, g_final, *, final):
    m = x.shape[0]
    tm = TM_DENSE
    n_j = D_FF // TF_FFN
    a = pl.pallas_call(
        _ffn_up_kernel,
        out_shape=jax.ShapeDtypeStruct((m, D_FF), BF16),
        grid=(m // tm, n_j),
        in_specs=[
            pl.BlockSpec((tm, D_MODEL), lambda i, j: (i, 0)),
            pl.BlockSpec((1, D_MODEL), lambda i, j: (0, 0)),
            pl.BlockSpec((None, D_MODEL, TF_FFN), lambda i, j: (layer, 0, j)),
            pl.BlockSpec((None, D_MODEL, TF_FFN), lambda i, j: (layer, 0, j + n_j)),
        ],
        out_specs=pl.BlockSpec((tm, TF_FFN), lambda i, j: (i, j)),
        scratch_shapes=[pltpu.VMEM((tm, D_MODEL), BF16)],
        compiler_params=_cparams(("parallel", "arbitrary")),
        name="ffn_up",
    )(x, g, w_in, w_in)
    y = pl.pallas_call(
        _ffn_down_kernel,
        out_shape=jax.ShapeDtypeStruct((m, D_MODEL), F32),
        grid=(m // tm, D_MODEL // TN_DENSE),
        in_specs=[
            pl.BlockSpec((tm, D_FF), lambda i, j: (i, 0)),
            pl.BlockSpec((D_FF, TN_DENSE), lambda i, j: (0, j)),
            pl.BlockSpec((tm, TN_DENSE), lambda i, j: (i, j)),
        ],
        out_specs=pl.BlockSpec((tm, TN_DENSE), lambda i, j: (i, j)),
        compiler_params=_cparams(("parallel", "arbitrary")),
        name="ffn_down",
    )(a, w_out, x)
    if not final:
        return y
    return pl.pallas_call(
        _norm_kernel,
        out_shape=jax.ShapeDtypeStruct((m, D_MODEL), F32),
        grid=(m // tm,),
        in_specs=[pl.BlockSpec((tm, D_MODEL), lambda i: (i, 0)), pl.BlockSpec((1, D_MODEL), lambda i: (0, 0))],
        out_specs=pl.BlockSpec((tm, D_MODEL), lambda i: (i, 0)),
        compiler_params=_cparams(("parallel",)),
        name="final_norm",
    )(y, g_final)


def _rope_cols(y, cos, sin):
    parts = []
    for hh in range(y.shape[1] // HEAD_DIM):
        yh = y[:, hh * HEAD_DIM:(hh + 1) * HEAD_DIM]
        parts.append(yh * cos + pltpu.roll(yh, HEAD_DIM // 2, 1) * sin)
    return parts[0] if len(parts) == 1 else jnp.concatenate(parts, axis=1)


def _proj_kernel(x_ref, g_ref, w_ref, cos_ref, sin_ref, o_ref, h_ref, *, mode, scale):
    j = pl.program_id(1)

    @pl.when(j == 0)
    def _():
        h_ref[...] = _rms(x_ref[...], g_ref[...]).astype(BF16)

    tm = o_ref.shape[0]
    rc = min(ROW_CHUNK, tm)

    def sweep(epilogue):
        for r0 in range(0, tm, rc):
            rs = slice(r0, r0 + rc)
            o_ref[rs, :] = epilogue(_dot(h_ref[rs, :], w_ref[...]), rs)

    rope = lambda y, rs: _rope_cols(y, cos_ref[rs, :], sin_ref[rs, :])
    if mode == "rope":
        sweep(lambda y, rs: rope(y, rs) * scale)
    elif mode == "rope_even":
        @pl.when(j % 2 == 0)
        def _():
            sweep(rope)

        @pl.when(j % 2 == 1)
        def _():
            sweep(lambda y, rs: y)
    elif mode == "sigmoid":
        sweep(lambda y, rs: jax.nn.sigmoid(y))
    else:
        sweep(lambda y, rs: y)


def _proj(x, g, w, col0, ncols, cos, sin, *, mode, scale=1.0, tn=TN_DENSE):
    m = x.shape[0]
    tm = min(TM_DENSE, m)
    tn = min(tn, ncols)
    jb0 = col0 // tn
    n_pos_tiles = cos.shape[0] // tm
    return pl.pallas_call(
        functools.partial(_proj_kernel, mode=mode, scale=scale),
        out_shape=jax.ShapeDtypeStruct((m, ncols), F32),
        grid=(m // tm, ncols // tn),
        in_specs=[
            pl.BlockSpec((tm, D_MODEL), lambda i, j: (i, 0)),
            pl.BlockSpec((1, D_MODEL), lambda i, j: (0, 0)),
            pl.BlockSpec((D_MODEL, tn), lambda i, j: (0, j + jb0)),
            pl.BlockSpec((tm, HEAD_DIM), lambda i, j: (i % n_pos_tiles, 0)),
            pl.BlockSpec((tm, HEAD_DIM), lambda i, j: (i % n_pos_tiles, 0)),
        ],
        out_specs=pl.BlockSpec((tm, tn), lambda i, j: (i, j)),
        scratch_shapes=[pltpu.VMEM((tm, D_MODEL), BF16)],
        compiler_params=_cparams(("parallel", "arbitrary")),
        name="proj_" + mode,
    )(x, g, w, cos, sin)


def _outproj_kernel(a_ref, w_ref, r_ref, o_ref):
    o_ref[...] = r_ref[...] + _dot(a_ref[...].astype(BF16), w_ref[...])


def _outproj(a, w, res):
    m, k = a.shape
    tm = min(TM_DENSE, m)
    tn = TN_DENSE
    return pl.pallas_call(
        _outproj_kernel,
        out_shape=jax.ShapeDtypeStruct((m, D_MODEL), F32),
        grid=(m // tm, D_MODEL // tn),
        in_specs=[
            pl.BlockSpec((tm, k), lambda i, j: (i, 0)),
            pl.BlockSpec((k, tn), lambda i, j: (0, j)),
            pl.BlockSpec((tm, tn), lambda i, j: (i, j)),
        ],
        out_specs=pl.BlockSpec((tm, tn), lambda i, j: (i, j)),
        compiler_params=_cparams(("parallel", "arbitrary")),
        name="outproj",
    )(a, w, res)


def _diff_lambda(lam, lam_init):
    a = jnp.sum(lam[0:1] * lam[1:2], axis=-1, keepdims=True)
    b = jnp.sum(lam[2:3] * lam[3:4], axis=-1, keepdims=True)
    return jnp.exp(a) - jnp.exp(b) + lam_init


def _subln(o, sub, lam_init):
    return _rms(o, sub) * (1.0 - lam_init)


def _diff_prompt_kernel(q_ref, k_ref, v_ref, lam_ref, sub_ref, o_ref, k1_s, k2_s, v_s, *, lam_init):
    k1_s[...] = k_ref[:, :HEAD_DIM].astype(BF16)
    k2_s[...] = k_ref[:, HEAD_DIM:].astype(BF16)
    v_s[...] = v_ref[...].astype(BF16)
    lmb = _diff_lambda(lam_ref[...], lam_init)
    sub = sub_ref[...]
    tq = TQ_DIFF
    hd2 = 2 * HEAD_DIM
    for qi in range(SEQ // tq):
        q0 = qi * tq
        kl = q0 + tq
        q1 = jnp.concatenate([q_ref[q0:q0 + tq, r * hd2:r * hd2 + HEAD_DIM]
                              for r in range(DIFF_REP)], axis=0).astype(BF16)
        q2 = jnp.concatenate([q_ref[q0:q0 + tq, r * hd2 + HEAD_DIM:(r + 1) * hd2]
                              for r in range(DIFF_REP)], axis=0).astype(BF16)
        rows = DIFF_REP * tq
        pos_q = q0 + (lax.broadcasted_iota(jnp.int32, (rows, kl), 0) & (tq - 1))
        mask = lax.broadcasted_iota(jnp.int32, (rows, kl), 1) <= pos_q
        p1 = _masked_softmax(_dot_nt(q1, k1_s[:kl]), mask)
        p2 = _masked_softmax(_dot_nt(q2, k2_s[:kl]), mask)
        a = (p1 - lmb * p2).astype(BF16)
        o = _subln(_dot(a, v_s[:kl]), sub, lam_init)
        for r in range(DIFF_REP):
            o_ref[q0:q0 + tq, r * hd2:(r + 1) * hd2] = o[r * tq:(r + 1) * tq].astype(BF16)


def _diff_prompt(q, kv, lam, sub, lam_init):
    gw = DIFF_REP * 2 * HEAD_DIM
    return pl.pallas_call(
        functools.partial(_diff_prompt_kernel, lam_init=lam_init),
        out_shape=jax.ShapeDtypeStruct((BATCH * SEQ, D_MODEL), BF16),
        grid=(BATCH, DIFF_KV_HEADS),
        in_specs=[
            pl.BlockSpec((SEQ, gw), lambda b, g: (b, g)),
            pl.BlockSpec((SEQ, 2 * HEAD_DIM), lambda b, g: (b, g)),
            pl.BlockSpec((SEQ, 2 * HEAD_DIM), lambda b, g: (b, DIFF_KV_HEADS + g)),
            pl.BlockSpec((4, HEAD_DIM), lambda b, g: (0, 0)),
            pl.BlockSpec((1, 2 * HEAD_DIM), lambda b, g: (0, 0)),
        ],
        out_specs=pl.BlockSpec((SEQ, gw), lambda b, g: (b, g)),
        scratch_shapes=[pltpu.VMEM((SEQ, HEAD_DIM), BF16), pltpu.VMEM((SEQ, HEAD_DIM), BF16),
                        pltpu.VMEM((SEQ, 2 * HEAD_DIM), BF16)],
        compiler_params=_cparams(("parallel", "parallel")),
        name="diff_prompt",
    )(q, kv, kv, lam, sub)


def _diff_sample_kernel(pt_ref, q_ref, kvn_ref, *rest, lam_init, n_steps):
    del pt_ref
    pages = rest[:DIFF_PAGES_PER_STEP]
    lam_ref, sub_ref, o_ref, m_s, l_s, acc_s = rest[DIFF_PAGES_PER_STEP:]
    step = pl.program_id(1)
    hd2 = 2 * HEAD_DIM
    nq = 2 * DIFF_HEADS
    ncol = PAGE_SIZE * SUBLANES

    @pl.when(step == 0)
    def _():
        m_s[...] = jnp.full(m_s.shape, -jnp.inf, F32)
        l_s[...] = jnp.zeros(l_s.shape, F32)
        acc_s[...] = jnp.zeros(acc_s.shape, F32)

    q = q_ref[...]
    qf = jnp.concatenate([q[:, h * hd2 + m * HEAD_DIM:h * hd2 + (m + 1) * HEAD_DIM]
                          for m in range(2) for h in range(DIFF_HEADS)], axis=0)
    qb = qf.astype(BF16)
    row = lax.broadcasted_iota(jnp.int32, (nq, ncol), 0)
    col = lax.broadcasted_iota(jnp.int32, (nq, ncol), 1)
    own = (((col >> 2) & 1) == (row >> 3)) & ((col & 3) == ((row >> 1) & 3))
    bias = jnp.where(own, 0.0, -jnp.inf)
    s_list = [_dot_nt(qb, pg[:, 0].reshape(ncol, HEAD_DIM).astype(BF16)) + bias for pg in pages]
    m_prev = m_s[:, :1]
    m_new = jnp.maximum(m_prev, jnp.max(functools.reduce(jnp.maximum, s_list), axis=-1, keepdims=True))
    corr = jnp.exp(m_prev - m_new)
    l_new = corr * l_s[:, :1]
    acc = jnp.concatenate([corr, corr], axis=0) * acc_s[...]
    for s, pg in zip(s_list, pages):
        p = jnp.exp(s - m_new)
        l_new = l_new + jnp.sum(p, axis=-1, keepdims=True)
        p1, p2 = p[:DIFF_HEADS], p[DIFF_HEADS:]
        p_lo = jnp.concatenate([p1, _roll_lanes(p2, LANES - DIFF_KV_HEADS)], axis=0)
        p_hi = jnp.concatenate([_roll_lanes(p1, DIFF_KV_HEADS), p2], axis=0)
        lhs = jnp.concatenate([p_lo, p_hi], axis=0).astype(BF16)
        acc = acc + _dot(lhs, pg[:, 1].reshape(ncol, HEAD_DIM).astype(BF16))
    m_s[...] = jnp.broadcast_to(m_new, m_s.shape)
    l_s[...] = jnp.broadcast_to(l_new, l_s.shape)
    acc_s[...] = acc

    @pl.when(step == n_steps - 1)
    def _():
        lmb = _diff_lambda(lam_ref[...], lam_init)
        kvn = kvn_ref[...]
        kv_off = DIFF_KV_HEADS * hd2
        head_rows = lambda f: jnp.concatenate([f(m, h // DIFF_REP) for m in range(2) for h in range(DIFF_HEADS)], 0)
        k_new = head_rows(lambda m, g: kvn[:, g * hd2 + m * HEAD_DIM:g * hd2 + (m + 1) * HEAD_DIM])
        v_lo = head_rows(lambda m, g: kvn[:, kv_off + g * hd2:kv_off + g * hd2 + HEAD_DIM])
        v_hi = head_rows(lambda m, g: kvn[:, kv_off + g * hd2 + HEAD_DIM:kv_off + (g + 1) * hd2])
        s_new = jnp.sum(qf * k_new, axis=-1, keepdims=True)
        m_old = m_s[:, :1]
        m_fin = jnp.maximum(m_old, s_new)
        c_old = jnp.exp(m_old - m_fin)
        p_new = jnp.exp(s_new - m_fin)
        inv = 1.0 / (c_old * l_s[:, :1] + p_new)
        on = jnp.concatenate([(c_old * acc_s[:nq] + p_new * v_lo) * inv,
                              (c_old * acc_s[nq:] + p_new * v_hi) * inv], axis=1)
        o = on[:DIFF_HEADS] - lmb * on[DIFF_HEADS:]
        o_ref[...] = _subln(o, sub_ref[...], lam_init)


def _diff_sample(q, kvn, cache, layer, page_table, lam, sub, lam_init):
    n_steps = N_PAGES // DIFF_PAGES_PER_STEP
    hd2 = 2 * HEAD_DIM
    nq = 2 * DIFF_HEADS
    page_specs = [
        pl.BlockSpec((None, None, PAGE_SIZE, 2, SUBLANES, HEAD_DIM),
                     functools.partial(lambda b, s, pt, k: (layer, pt[b, s * DIFF_PAGES_PER_STEP + k], 0, 0, 0, 0),
                                       k=k))
        for k in range(DIFF_PAGES_PER_STEP)
    ]
    grid_spec = pltpu.PrefetchScalarGridSpec(
        num_scalar_prefetch=1,
        grid=(DEC_BATCH, n_steps),
        in_specs=[
            pl.BlockSpec((None, 1, D_MODEL), lambda b, s, pt: (b, 0, 0)),
            pl.BlockSpec((None, 1, D_MODEL), lambda b, s, pt: (b, 0, 0)),
            *page_specs,
            pl.BlockSpec((4, HEAD_DIM), lambda b, s, pt: (0, 0)),
            pl.BlockSpec((1, hd2), lambda b, s, pt: (0, 0)),
        ],
        out_specs=pl.BlockSpec((None, DIFF_HEADS, hd2), lambda b, s, pt: (b, 0, 0)),
        scratch_shapes=[pltpu.VMEM((nq, LANES), F32), pltpu.VMEM((nq, LANES), F32),
                        pltpu.VMEM((2 * nq, HEAD_DIM), F32)],
    )
    return pl.pallas_call(
        functools.partial(_diff_sample_kernel, lam_init=lam_init, n_steps=n_steps),
        out_shape=jax.ShapeDtypeStruct((DEC_BATCH, DIFF_HEADS, hd2), F32),
        grid_spec=grid_spec,
        compiler_params=_cparams(("parallel", "arbitrary")),
        name="diff_sample",
    )(page_table, q, kvn, *([cache] * DIFF_PAGES_PER_STEP), lam, sub)


def _compress_kernel(*refs, n_src, n_prefetch):
    refs = refs[n_prefetch:]
    srcs = refs[:n_src]
    wcat_ref, pe_ref, b1_ref, w2_ref, o_ref, x_s, carry_s = refs[n_src:]
    step = pl.program_id(1)

    @pl.when(step == 0)
    def _():
        carry_s[...] = jnp.zeros(carry_s.shape, F32)

    planes = 2 * NSA_GROUPS
    nc = CMP_CHUNKS_PER_SLAB // CMP_PARTS
    row0 = lax.broadcasted_iota(jnp.int32, (nc, CMP_HIDDEN), 0) == 0
    rows_per_src = CMP_ROWS_PER_SLAB // n_src
    tiles = [(src, t0) for src in srcs for t0 in range(0, rows_per_src, SUBLANES * CMP_STRIDE)]
    tiles_per_part = len(tiles) // CMP_PARTS
    biases = []
    for n in range(2):
        pb = _dot(pe_ref[n].astype(BF16), wcat_ref[n])
        biases.append(pb[0:1, :CMP_HIDDEN] + pb[1:2, CMP_HIDDEN:] + b1_ref[n])
    for part in range(CMP_PARTS):
        part_tiles = tiles[part * tiles_per_part:(part + 1) * tiles_per_part]
        for tp in range(0, tiles_per_part, 2):
            c0 = tp * SUBLANES
            for r in range(CMP_STRIDE):
                ya, yb = [jnp.swapaxes(src[pl.ds(t0 + r, SUBLANES, stride=CMP_STRIDE)], 0, 1)
                          for src, t0 in part_tiles[tp:tp + 2]]
                for p in range(planes):
                    blk = jnp.concatenate([ya[p], yb[p]], axis=0).astype(BF16)
                    g = p % NSA_GROUPS
                    x_s[part, p // NSA_GROUPS, g * nc + c0:g * nc + c0 + 2 * SUBLANES,
                        r * HEAD_DIM:(r + 1) * HEAD_DIM] = blk
        for n in range(2):
            fs = _dot(x_s[part, n], wcat_ref[n])
            for g in range(NSA_GROUPS):
                first = fs[g * nc:(g + 1) * nc, :CMP_HIDDEN]
                second = fs[g * nc:(g + 1) * nc, CMP_HIDDEN:]
                prev_first = jnp.where(row0, carry_s[n, g][0:1], pltpu.roll(first, 1, 0))
                carry_s[n, g] = jnp.broadcast_to(first[nc - 1:nc], (SUBLANES, CMP_HIDDEN))
                hid = _silu(prev_first + second + biases[n])
                o_ref[n, g, part * nc:(part + 1) * nc, :] = _dot(hid.astype(BF16), w2_ref[n])


def _compress_specs(idx):
    return [
        pl.BlockSpec((2, CMP_STRIDE * HEAD_DIM, 2 * CMP_HIDDEN), idx(lambda: (0, 0, 0))),
        pl.BlockSpec((2, SUBLANES, CMP_STRIDE * HEAD_DIM), idx(lambda: (0, 0, 0))),
        pl.BlockSpec((2, 1, CMP_HIDDEN), idx(lambda: (0, 0, 0))),
        pl.BlockSpec((2, CMP_HIDDEN, HEAD_DIM), idx(lambda: (0, 0, 0))),
    ]


_CMP_SCRATCH = [pltpu.VMEM((CMP_PARTS, 2, NSA_GROUPS * CMP_CHUNKS_PER_SLAB // CMP_PARTS, CMP_STRIDE * HEAD_DIM), BF16),
                pltpu.VMEM((2, NSA_GROUPS, SUBLANES, CMP_HIDDEN), F32)]


def _compress_prompt(rows, cw):
    n_slab = SEQ // CMP_ROWS_PER_SLAB
    planes = 2 * NSA_GROUPS
    return pl.pallas_call(
        functools.partial(_compress_kernel, n_src=1, n_prefetch=0),
        out_shape=jax.ShapeDtypeStruct((BATCH, 2, NSA_GROUPS, n_slab * CMP_CHUNKS_PER_SLAB, HEAD_DIM), F32),
        grid=(BATCH, n_slab),
        in_specs=[pl.BlockSpec((CMP_ROWS_PER_SLAB, planes, HEAD_DIM), lambda b, s: (b * n_slab + s, 0, 0))]
        + _compress_specs(lambda f: (lambda b, s: f())),
        out_specs=pl.BlockSpec((None, 2, NSA_GROUPS, CMP_CHUNKS_PER_SLAB, HEAD_DIM), lambda b, s: (b, 0, 0, s, 0)),
        scratch_shapes=_CMP_SCRATCH,
        compiler_params=_cparams(("parallel", "arbitrary")),
        name="compress_prompt",
    )(rows, *cw)


def _compress_sample(cache, layer, page_table, cw):
    n_slab = PAST_LEN // CMP_ROWS_PER_SLAB
    planes = 2 * NSA_GROUPS
    page_specs = [
        pl.BlockSpec((None, None, PAGE_SIZE, planes, HEAD_DIM),
                     functools.partial(lambda b, s, pt, k: (layer, pt[b, s * CMP_PAGES_PER_SLAB + k], 0, 0, 0), k=k))
        for k in range(CMP_PAGES_PER_SLAB)
    ]
    grid_spec = pltpu.PrefetchScalarGridSpec(
        num_scalar_prefetch=1,
        grid=(DEC_BATCH, n_slab),
        in_specs=page_specs + _compress_specs(lambda f: (lambda b, s, pt: f())),
        out_specs=pl.BlockSpec((None, 2, NSA_GROUPS, CMP_CHUNKS_PER_SLAB, HEAD_DIM),
                               lambda b, s, pt: (b, 0, 0, s, 0)),
        scratch_shapes=_CMP_SCRATCH,
    )
    return pl.pallas_call(
        functools.partial(_compress_kernel, n_src=CMP_PAGES_PER_SLAB, n_prefetch=1),
        out_shape=jax.ShapeDtypeStruct((DEC_BATCH, 2, NSA_GROUPS, n_slab * CMP_CHUNKS_PER_SLAB, HEAD_DIM), F32),
        grid_spec=grid_spec,
        compiler_params=_cparams(("parallel", "arbitrary")),
        name="compress_sample",
    )(page_table, *([cache] * CMP_PAGES_PER_SLAB), *cw)


def _topk_rank(score, blk, n_blocks):
    rank = jnp.zeros(score.shape, F32)
    for sp in range(n_blocks):
        c = score[:, sp:sp + 1]
        before = (c > score) | ((c == score) & (blk > sp))
        rank = rank + jnp.where(before, 1.0, 0.0)
    return rank


def _nsa_prompt_kernel(q_ref, kvc_ref, ks_ref, vs_ref, kw_ref, vw_ref, gt_ref, cover_ref, expand_ref,
                       o_ref, ks_s, vs_s, kw_s, vw_s):
    ks_s[...] = ks_ref[...].astype(BF16)
    vs_s[...] = vs_ref[...].astype(BF16)
    kw_s[...] = kw_ref[...].astype(BF16)
    vw_s[...] = vw_ref[...].astype(BF16)
    kc = kvc_ref[0].astype(BF16)
    vc = kvc_ref[1].astype(BF16)
    n_ent = kc.shape[0]
    tq = TQ_NSA
    rows = NSA_REP * tq
    for qi in range(SEQ // tq):
        q0 = qi * tq
        kl = q0 + tq
        qb = jnp.concatenate([q_ref[q0:q0 + tq, r * HEAD_DIM:(r + 1) * HEAD_DIM]
                              for r in range(NSA_REP)], axis=0).astype(BF16)

        pos_c = q0 + lax.broadcasted_iota(jnp.int32, (tq, n_ent), 0)
        ent = lax.broadcasted_iota(jnp.int32, (tq, n_ent), 1)
        mask_c = (ent >= 1) & (ent * CMP_STRIDE + (CMP_LEN - 1 - CMP_STRIDE) <= pos_c)
        p_c = _biased_softmax(_dot_nt(qb, kc), _head_bias(mask_c))
        o_c = _dot(p_c.astype(BF16), vc)

        psum = functools.reduce(lambda a, b: a + b, [p_c[r * tq:(r + 1) * tq] for r in range(NSA_REP)])
        imp = _dot(psum.astype(BF16), cover_ref[...])
        blk = lax.broadcasted_iota(jnp.int32, (tq, LANES), 1)
        cur = (q0 + lax.broadcasted_iota(jnp.int32, (tq, LANES), 0)) >> SEL_SHIFT
        n_blk = kl // SEL_LEN
        if n_blk <= SEL_TOPK:
            sel = jnp.where(blk <= cur, 1.0, 0.0)
        else:
            forced = (blk == 0) | (blk == cur) | (blk == cur - 1)
            score = jnp.where(forced, SEL_FORCE, jnp.where(blk <= cur, imp, -SEL_FORCE))
            rank = _topk_rank(score, blk, n_blk)
            sel = jnp.where((rank < SEL_TOPK) & (blk <= cur), 1.0, 0.0)
        selk = _dot(sel.astype(BF16), expand_ref[:, :kl])
        pos_s = q0 + lax.broadcasted_iota(jnp.int32, (tq, kl), 0)
        key_s = lax.broadcasted_iota(jnp.int32, (tq, kl), 1)
        mask_s = (selk > 0.5) & (key_s <= pos_s)
        p_s = _biased_softmax(_dot_nt(qb, ks_s[:kl]), _head_bias(mask_s))
        o_s = _dot(p_s.astype(BF16), vs_s[:kl])

        w0 = max(0, q0 - WINDOW)
        wl = kl - w0
        rel = (q0 - w0) + lax.broadcasted_iota(jnp.int32, (tq, wl), 0) - lax.broadcasted_iota(jnp.int32, (tq, wl), 1)
        mask_w = (rel >= 0) & (rel < WINDOW)
        p_w = _biased_softmax(_dot_nt(qb, kw_s[w0:kl]), _head_bias(mask_w))
        o_w = _dot(p_w.astype(BF16), vw_s[w0:kl])

        gt = gt_ref[q0:q0 + tq, :]
        for r in range(NSA_REP):
            sl = slice(r * tq, (r + 1) * tq)
            o = (gt[:, 3 * r:3 * r + 1] * o_c[sl] + gt[:, 3 * r + 1:3 * r + 2] * o_s[sl]
                 + gt[:, 3 * r + 2:3 * r + 3] * o_w[sl])
            o_ref[q0:q0 + tq, r * HEAD_DIM:(r + 1) * HEAD_DIM] = o.astype(BF16)


def _nsa_prompt(q, kvc, sel_rows, win_rows, gates, cover, expand):
    gw = NSA_REP * HEAD_DIM
    n_ent = kvc.shape[3]
    row_spec_k = pl.BlockSpec((SEQ, HEAD_DIM), lambda b, g: (b, g))
    row_spec_v = pl.BlockSpec((SEQ, HEAD_DIM), lambda b, g: (b, NSA_GROUPS + g))
    return pl.pallas_call(
        _nsa_prompt_kernel,
        out_shape=jax.ShapeDtypeStruct((BATCH * SEQ, D_MODEL), BF16),
        grid=(BATCH, NSA_GROUPS),
        in_specs=[
            pl.BlockSpec((SEQ, gw), lambda b, g: (b, g)),
            pl.BlockSpec((None, 2, None, n_ent, HEAD_DIM), lambda b, g: (b, 0, g, 0, 0)),
            row_spec_k, row_spec_v, row_spec_k, row_spec_v,
            pl.BlockSpec((SEQ, LANES), lambda b, g: (b, g)),
            pl.BlockSpec(cover.shape, lambda b, g: (0, 0)),
            pl.BlockSpec(expand.shape, lambda b, g: (0, 0)),
        ],
        out_specs=pl.BlockSpec((SEQ, gw), lambda b, g: (b, g)),
        scratch_shapes=[pltpu.VMEM((SEQ, HEAD_DIM), BF16)] * 4,
        compiler_params=_cparams(("parallel", "parallel")),
        name="nsa_prompt",
    )(q, kvc, sel_rows, sel_rows, win_rows, win_rows, gates, cover, expand)


def _head_rows(q):
    parts = [q[:, r * HEAD_DIM:(r + 1) * HEAD_DIM] for r in range(NSA_REP)]
    parts.append(jnp.zeros((SUBLANES - NSA_REP, HEAD_DIM), F32))
    return jnp.concatenate(parts, axis=0)


def _nsa_sample_cmp_kernel(q_ref, kvc_ref, cover_ref, o_ref, idx_ref):
    n_ent = kvc_ref.shape[2]
    ent = lax.broadcasted_iota(jnp.int32, (SUBLANES, n_ent), 1)
    mask_c = ent >= 1
    psums = []
    for bi in range(J1_BATCH):
        qh = _head_rows(q_ref[bi:bi + 1, :])
        p = _masked_softmax(_dot_nt(qh.astype(BF16), kvc_ref[bi, 0].astype(BF16)), mask_c)
        o = _dot(p.astype(BF16), kvc_ref[bi, 1].astype(BF16))
        for r in range(NSA_REP):
            o_ref[bi:bi + 1, r * HEAD_DIM:(r + 1) * HEAD_DIM] = o[r:r + 1]
        psums.append(functools.reduce(lambda a, b: a + b, [p[r:r + 1] for r in range(NSA_REP)]))
    psum = jnp.concatenate(psums, axis=0)
    imp = _dot(psum.astype(BF16), cover_ref[...])
    blk = lax.broadcasted_iota(jnp.int32, imp.shape, 1)
    cur = N_SEL_PAST
    forced = (blk == 0) | (blk == cur) | (blk == cur - 1)
    score = jnp.where(forced, SEL_FORCE, jnp.where(blk <= cur, imp, -SEL_FORCE))
    rank = _topk_rank(score, blk, cur + 1)
    blk_f = blk.astype(F32)
    lane = lax.broadcasted_iota(jnp.int32, (J1_BATCH, LANES), 1)
    out = jnp.zeros((J1_BATCH, LANES), F32)
    for k in range(SEL_TOPK):
        ik = jnp.sum(jnp.where(rank == float(k), blk_f, 0.0), axis=-1, keepdims=True)
        out = out + jnp.where(lane == k, ik, 0.0)
    idx_ref[...] = out.astype(jnp.int32)


def _nsa_sample_cmp(q, kvc, cover):
    gw = NSA_REP * HEAD_DIM
    n_ent = kvc.shape[3]
    return pl.pallas_call(
        _nsa_sample_cmp_kernel,
        out_shape=(jax.ShapeDtypeStruct((DEC_BATCH, D_MODEL), F32),
                   jax.ShapeDtypeStruct((NSA_GROUPS, DEC_BATCH, LANES), jnp.int32)),
        grid=(NSA_GROUPS, DEC_BATCH // J1_BATCH),
        in_specs=[
            pl.BlockSpec((J1_BATCH, gw), lambda g, c: (c, g)),
            pl.BlockSpec((J1_BATCH, 2, None, n_ent, HEAD_DIM), lambda g, c: (c, 0, g, 0, 0)),
            pl.BlockSpec(cover.shape, lambda g, c: (0, 0)),
        ],
        out_specs=(pl.BlockSpec((J1_BATCH, gw), lambda g, c: (c, g)),
                   pl.BlockSpec((None, J1_BATCH, LANES), lambda g, c: (g, c, 0))),
        compiler_params=_cparams(("parallel", "parallel")),
        name="nsa_sample_cmp",
    )(q, kvc, cover)


def _nsa_sample_kernel(pt_ref, idx_ref, q_ref, oc_ref, gt_ref, ksn_ref, vsn_ref, kwn_ref, vwn_ref,
                       win_ref, *rest):
    del pt_ref, idx_ref
    blocks = rest[:SEL_PAST_SLOTS]
    o_ref = rest[SEL_PAST_SLOTS]
    g = pl.program_id(1)
    planes = 2 * NSA_GROUPS
    qh = _head_rows(q_ref[...])
    qb = qh.astype(BF16)

    def own_bias(n_rows, first_token):
        col = lax.broadcasted_iota(jnp.int32, (SUBLANES, n_rows), 1)
        return jnp.where(((col & (planes - 1)) == g) & (col >= first_token * planes), 0.0, -jnp.inf)

    to_values = lambda e: _roll_lanes(e, NSA_GROUPS).astype(BF16)

    bias_s = own_bias(SEL_LEN * planes, 0)
    kv_list = [blk[...].astype(BF16) for blk in blocks]
    s_list = [_dot_nt(qb, kv) + bias_s for kv in kv_list]
    s_new = jnp.sum(qh * ksn_ref[...], axis=-1, keepdims=True)
    m = jnp.maximum(jnp.max(functools.reduce(jnp.maximum, s_list), axis=-1, keepdims=True), s_new)
    e_new = jnp.exp(s_new - m)
    den = e_new
    acc = e_new * vsn_ref[...]
    for s, kv in zip(s_list, kv_list):
        e = jnp.exp(s - m)
        den = den + jnp.sum(e, axis=-1, keepdims=True)
        acc = acc + _dot(to_values(e), kv)
    o_s = acc * (1.0 / jnp.maximum(den, 1e-30))

    win = win_ref[...].astype(BF16)
    s_w = _dot_nt(qb, win) + own_bias(win.shape[0], 1)
    s_wn = jnp.sum(qh * kwn_ref[...], axis=-1, keepdims=True)
    m_w = jnp.maximum(jnp.max(s_w, axis=-1, keepdims=True), s_wn)
    e_w = jnp.exp(s_w - m_w)
    e_wn = jnp.exp(s_wn - m_w)
    den_w = jnp.sum(e_w, axis=-1, keepdims=True) + e_wn
    o_w = (_dot(to_values(e_w), win) + e_wn * vwn_ref[...]) * (1.0 / jnp.maximum(den_w, 1e-30))

    gt = gt_ref[...]
    for r in range(NSA_REP):
        o = (gt[:, 3 * r:3 * r + 1] * oc_ref[:, r * HEAD_DIM:(r + 1) * HEAD_DIM]
             + gt[:, 3 * r + 1:3 * r + 2] * o_s[r:r + 1] + gt[:, 3 * r + 2:3 * r + 3] * o_w[r:r + 1])
        o_ref[:, r * HEAD_DIM:(r + 1) * HEAD_DIM] = o


def _nsa_sample(q, o_c, gates, sel_new, win_new, win_state, cache_sel, layer, page_table, idx):
    gw = NSA_REP * HEAD_DIM
    planes = 2 * NSA_GROUPS
    sub = PAGE_SIZE // SEL_LEN
    slots = [0, 1] + list(range(3, SEL_TOPK))

    def half_page(b, g, pt, ix, slot):
        blk = jnp.minimum(ix[b, g * SEL_TOPK + slot], N_SEL_PAST - 1)
        return pt[b, blk // sub] * sub + blk % sub

    def block_spec(slot):
        return pl.BlockSpec((None, None, SEL_LEN * planes, HEAD_DIM),
                            lambda b, g, pt, ix: (layer, half_page(b, g, pt, ix, slot), 0, 0))

    row = lambda voff: pl.BlockSpec((None, 1, HEAD_DIM), lambda b, g, pt, ix: (b, 0, voff + g))
    grid_spec = pltpu.PrefetchScalarGridSpec(
        num_scalar_prefetch=2,
        grid=(DEC_BATCH, NSA_GROUPS),
        in_specs=[
            pl.BlockSpec((None, 1, gw), lambda b, g, pt, ix: (b, 0, g)),
            pl.BlockSpec((None, 1, gw), lambda b, g, pt, ix: (b, 0, g)),
            pl.BlockSpec((None, 1, LANES), lambda b, g, pt, ix: (b, 0, g)),
            row(0), row(NSA_GROUPS), row(0), row(NSA_GROUPS),
            pl.BlockSpec((None, None, win_state.shape[2], HEAD_DIM), lambda b, g, pt, ix: (layer, b, 0, 0)),
            *[block_spec(s) for s in slots],
        ],
        out_specs=pl.BlockSpec((None, 1, gw), lambda b, g, pt, ix: (b, 0, g)),
    )
    return pl.pallas_call(
        _nsa_sample_kernel,
        out_shape=jax.ShapeDtypeStruct((DEC_BATCH, 1, D_MODEL), F32),
        grid_spec=grid_spec,
        compiler_params=_cparams(("parallel", "arbitrary")),
        name="nsa_sample",
    )(page_table, idx, q, o_c, gates, sel_new, sel_new, win_new, win_new, win_state,
      *([cache_sel] * SEL_PAST_SLOTS))


def _rope_tables(pos):
    half = HEAD_DIM // 2
    inv = ROPE_THETA ** (-jnp.arange(half, dtype=F32) / half)
    ang = pos.astype(F32)[:, None] * inv[None, :]
    cos, sin = jnp.cos(ang), jnp.sin(ang)
    return jnp.concatenate([cos, cos], axis=-1), jnp.concatenate([-sin, sin], axis=-1)


def _cover_table(n_ent, n_sel, n_lanes):
    start = (jnp.arange(n_ent)[:, None] - 1) * CMP_STRIDE
    bstart = jnp.arange(n_lanes)[None, :] * SEL_LEN
    hit = (start < bstart + SEL_LEN) & (start + CMP_LEN > bstart)
    hit = hit & (jnp.arange(n_ent)[:, None] >= 1) & (jnp.arange(n_lanes)[None, :] < n_sel)
    return hit.astype(BF16)


def _expand_table(n_keys):
    return (jnp.arange(LANES)[:, None] == (jnp.arange(n_keys)[None, :] // SEL_LEN)).astype(BF16)


def kernel(x_prompt, x_sample, cache_diff, cache_nsa_cmp, cache_nsa_sel, state_nsa_win, page_table,
           norm_ffn1, w_ffn1_in, w_ffn1_out, norm_mix, norm_ffn2, w_ffn2_in, w_ffn2_out,
           w_diff_qkv, diff_lambda, diff_subln, w_diff_o,
           w_nsa_in, nsa_cmp_pe, nsa_cmp_w1, nsa_cmp_b1, nsa_cmp_w2, w_nsa_o, norm_final):
    xp = x_prompt.reshape(BATCH * SEQ, D_MODEL)
    xs = x_sample.reshape(DEC_BATCH * DEC_SEQ, D_MODEL)
    n_pool = cache_diff.shape[1]
    wb = state_nsa_win.shape[2]
    gw2 = 2 * NSA_GROUPS * HEAD_DIM
    cos_p, sin_p = _rope_tables(jnp.arange(SEQ))
    cos_s, sin_s = _rope_tables(jnp.full((DEC_BATCH,), PAST_LEN))
    qscale = HEAD_DIM ** -0.5
    row2 = lambda v: v.reshape(1, -1)

    outs = {k: [] for k in ("diff_p", "diff_s", "cmp_p", "cmp_s", "sel_p", "sel_s", "win_p", "win_s")}
    for i in range(DEPTH):
        last = i == DEPTH - 1
        layer = i // 2
        xp = _ffn_prompt(xp, row2(norm_ffn1[i]), w_ffn1_in, i, w_ffn1_out[i].astype(BF16),
                  row2(norm_final), final=False)
        xs = _ffn(xs, row2(norm_ffn1[i]), w_ffn1_in, i, w_ffn1_out[i].astype(BF16),
                  row2(norm_final), final=False)
        gmix = row2(norm_mix[i])
        if i % 2 == 0:
            lam_init = 0.8 - 0.6 * math.exp(-0.3 * i)
            w = w_diff_qkv[layer].astype(BF16)
            qd = 2 * DIFF_HEADS * HEAD_DIM
            kd = 2 * DIFF_KV_HEADS * HEAD_DIM
            lam = diff_lambda[layer]
            sub = row2(diff_subln[layer])
            wo = w_diff_o[layer].astype(BF16)
            res = []
            for x, cos, sin in ((xp, cos_p, sin_p), (xs, cos_s, sin_s)):
                q = _proj(x, gmix, w, 0, qd, cos, sin, mode="rope", scale=qscale)
                kv = _proj(x, gmix, w, qd, 2 * kd, cos, sin, mode="rope_even", tn=kd)
                res.append((q, kv))
            (qp, kvp), (qs, kvs) = res
            op = _diff_prompt(qp, kvp, lam, sub, lam_init)
            cache = cache_diff.reshape(cache_diff.shape[0], n_pool, PAGE_SIZE, 2, DIFF_KV_HEADS, 2, HEAD_DIM)
            cache = jnp.transpose(cache, (0, 1, 2, 3, 5, 4, 6)).reshape(
                cache_diff.shape[0], n_pool, PAGE_SIZE, 2, 2 * DIFF_KV_HEADS, HEAD_DIM)
            os_ = _diff_sample(qs.reshape(DEC_BATCH, 1, D_MODEL), kvs.reshape(DEC_BATCH, 1, D_MODEL), cache, layer,
                               page_table, lam, sub, lam_init).reshape(DEC_BATCH, D_MODEL)
            xp = _outproj(op, wo, xp)
            xs = _outproj(os_, wo, xs)
            outs["diff_p"].append(kvp.reshape(BATCH, SEQ, 2, DIFF_KV_HEADS, 2 * HEAD_DIM))
            outs["diff_s"].append(kvs.reshape(DEC_BATCH, DEC_SEQ, 2, DIFF_KV_HEADS, 2 * HEAD_DIM))
        else:
            w_in = w_nsa_in[layer]
            w = w_in.astype(BF16)
            qd = NSA_HEADS * HEAD_DIM
            wg = w_in[:, qd + 3 * gw2:].reshape(D_MODEL, NSA_GROUPS, NSA_REP * 3)
            wg = jnp.pad(wg, ((0, 0), (0, 0), (0, LANES - NSA_REP * 3))).reshape(D_MODEL, NSA_GROUPS * LANES)
            wg = wg.astype(BF16)
            wo = w_nsa_o[layer].astype(BF16)
            w1 = nsa_cmp_w1[layer]
            wcat = jnp.concatenate([w1[:, :CMP_STRIDE].reshape(2, CMP_STRIDE * HEAD_DIM, CMP_HIDDEN),
                                    w1[:, CMP_STRIDE:].reshape(2, CMP_STRIDE * HEAD_DIM, CMP_HIDDEN)],
                                   axis=-1).astype(BF16)
            pe = nsa_cmp_pe[layer].reshape(2, 2, CMP_STRIDE * HEAD_DIM)
            pe = jnp.pad(pe, ((0, 0), (0, SUBLANES - 2), (0, 0)))
            cw = (wcat, pe, nsa_cmp_b1[layer].reshape(2, 1, CMP_HIDDEN), nsa_cmp_w2[layer].astype(BF16))
            res = []
            for x, cos, sin in ((xp, cos_p, sin_p), (xs, cos_s, sin_s)):
                q = _proj(x, gmix, w, 0, qd, cos, sin, mode="rope", scale=qscale)
                rows = [_proj(x, gmix, w, qd + c * gw2, gw2, cos, sin, mode="rope_even") for c in range(3)]
                gates = _proj(x, gmix, wg, 0, NSA_GROUPS * LANES, cos, sin, mode="sigmoid")
                res.append((q, rows, gates))
            (qp, rows_p, gates_p), (qs, rows_s, gates_s) = res

            kvc_p = _compress_prompt(rows_p[0].reshape(BATCH * SEQ, 2 * NSA_GROUPS, HEAD_DIM), cw)
            n_ent_p = kvc_p.shape[3]
            op = _nsa_prompt(qp, kvc_p, rows_p[1], rows_p[2], gates_p,
                             _cover_table(n_ent_p, SEQ // SEL_LEN, LANES), _expand_table(SEQ))

            cache_c = cache_nsa_cmp.reshape(cache_nsa_cmp.shape[0], n_pool, PAGE_SIZE, 2 * NSA_GROUPS, HEAD_DIM)
            kvc_s = _compress_sample(cache_c, layer, page_table, cw)
            o_c, idx = _nsa_sample_cmp(qs, kvc_s, _cover_table(kvc_s.shape[3], N_SEL_PAST + 1, 2 * LANES))
            idx = jnp.transpose(idx[:, :, :SEL_TOPK], (1, 0, 2)).reshape(DEC_BATCH, NSA_GROUPS * SEL_TOPK)
            planes = 2 * NSA_GROUPS
            cache_s = cache_nsa_sel.reshape(cache_nsa_sel.shape[0], n_pool * (PAGE_SIZE // SEL_LEN),
                                            SEL_LEN * planes, HEAD_DIM)
            win_state = state_nsa_win.reshape(state_nsa_win.shape[0], DEC_BATCH, wb * planes, HEAD_DIM)
            r3 = lambda a: a.reshape(DEC_BATCH, 1, a.shape[-1])
            os_ = _nsa_sample(r3(qs), r3(o_c), r3(gates_s), r3(rows_s[1]), r3(rows_s[2]), win_state, cache_s,
                              layer, page_table, idx).reshape(DEC_BATCH, D_MODEL)
            xp = _outproj(op, wo, xp)
            xs = _outproj(os_, wo, xs)
            shp = (2, NSA_GROUPS, HEAD_DIM)
            outs["cmp_p"].append(rows_p[0].reshape(BATCH, SEQ, *shp))
            outs["cmp_s"].append(rows_s[0].reshape(DEC_BATCH, DEC_SEQ, *shp))
            outs["sel_p"].append(rows_p[1].reshape(BATCH, SEQ, *shp))
            outs["sel_s"].append(rows_s[1].reshape(DEC_BATCH, DEC_SEQ, *shp))
            wp = rows_p[2].reshape(BATCH, SEQ, *shp)
            outs["win_p"].append(wp[:, SEQ - min(WINDOW, SEQ):])
            kvw = jnp.concatenate([state_nsa_win[layer], rows_s[2].reshape(DEC_BATCH, DEC_SEQ, *shp)], axis=1)
            outs["win_s"].append(kvw[:, DEC_SEQ:])
        xp = _ffn_prompt(xp, row2(norm_ffn2[i]), w_ffn2_in, i, w_ffn2_out[i].astype(BF16),
                  row2(norm_final), final=last)
        xs = _ffn(xs, row2(norm_ffn2[i]), w_ffn2_in, i, w_ffn2_out[i].astype(BF16),
                  row2(norm_final), final=last)
    st = lambda k: jnp.stack(outs[k], axis=0)
    return (xp.reshape(BATCH, SEQ, D_MODEL), xs.reshape(DEC_BATCH, DEC_SEQ, D_MODEL),
            st("diff_p"), st("diff_s"), st("cmp_p"), st("cmp_s"), st("sel_p"), st("sel_s"),
            st("win_p"), st("win_s"))
```

```python
import functools
import math

import jax
import jax.numpy as jnp
from jax import lax
from jax.experimental import pallas as pl
from jax.experimental.pallas import tpu as pltpu

F32 = jnp.float32
BF16 = jnp.bfloat16

D_MODEL = 2048
BATCH = 4
SEQ = 2048
DEPTH = 2
DEC_BATCH = 32
DEC_SEQ = 1
PAST_LEN = 8192
PAGE_SIZE = 128
HEAD_DIM = 128
ROPE_THETA = 10000.0
NORM_EPS = 1e-6
D_FF = 5632
DIFF_HEADS = D_MODEL // (2 * HEAD_DIM)
DIFF_KV_HEADS = 4
DIFF_REP = DIFF_HEADS // DIFF_KV_HEADS
NSA_HEADS = D_MODEL // HEAD_DIM
NSA_GROUPS = 4
NSA_REP = NSA_HEADS // NSA_GROUPS
CMP_LEN = 32
CMP_STRIDE = 16
CMP_HIDDEN = 256
SEL_LEN = 64
SEL_SHIFT = 6
SEL_TOPK = 16
WINDOW = 512
SEL_FORCE = 1e30

N_PAGES = PAST_LEN // PAGE_SIZE
LANES = 128
SUBLANES = 8
VMEM_LIMIT = 56 * 1024 * 1024

TM_DENSE = 1024
TM_FFN = 512
TF_FFN = 512
TN_DENSE = 512
ROW_CHUNK = 256
TQ_DIFF = 256
TQ_NSA = 128
DIFF_PAGES_PER_STEP = 8
CMP_ROWS_PER_SLAB = 2048
CMP_CHUNKS_PER_SLAB = CMP_ROWS_PER_SLAB // CMP_STRIDE
CMP_PAGES_PER_SLAB = CMP_ROWS_PER_SLAB // PAGE_SIZE
CMP_PARTS = 2
N_SEL_PAST = PAST_LEN // SEL_LEN
J1_BATCH = 8
SEL_PAST_SLOTS = SEL_TOPK - 1


def _cparams(sem, vmem=VMEM_LIMIT):
    return pltpu.CompilerParams(dimension_semantics=sem, vmem_limit_bytes=vmem)


def _rms(x, g):
    ms = jnp.mean(x * x, axis=-1, keepdims=True)
    return x * lax.rsqrt(ms + NORM_EPS) * g


def _silu(x):
    return x * jax.nn.sigmoid(x)


def _dot(a, b):
    return jnp.dot(a, b, preferred_element_type=F32)


def _dot_nt(a, b):
    return lax.dot_general(a, b, (((1,), (1,)), ((), ())), preferred_element_type=F32)


def _roll_lanes(x, shift):
    n = x.shape[1] // LANES
    parts = [pltpu.roll(x[:, c * LANES:(c + 1) * LANES], shift, 1) for c in range(n)]
    return parts[0] if n == 1 else jnp.concatenate(parts, axis=1)


def _head_bias(mask):
    bias = jnp.where(mask, 0.0, -jnp.inf)
    return jnp.concatenate([bias] * NSA_REP, axis=0)


def _biased_softmax(s, bias):
    s = s + bias
    m = jnp.max(s, axis=-1, keepdims=True)
    m = jnp.where(jnp.isfinite(m), m, 0.0)
    e = jnp.exp(s - m)
    inv = 1.0 / jnp.maximum(jnp.sum(e, axis=-1, keepdims=True), 1e-30)
    return e * inv


def _masked_softmax(s, mask):
    s = jnp.where(mask, s, -jnp.inf)
    m = jnp.max(s, axis=-1, keepdims=True)
    m = jnp.where(jnp.isfinite(m), m, 0.0)
    e = jnp.exp(s - m)
    inv = 1.0 / jnp.maximum(jnp.sum(e, axis=-1, keepdims=True), 1e-30)
    return e * inv


def _ffn_kernel(x_ref, g_ref, wg_ref, wu_ref, wo_ref, gf_ref, o_ref, h_ref, *, n_j, final):
    j = pl.program_id(1)

    @pl.when(j == 0)
    def _():
        x = x_ref[...]
        h_ref[...] = _rms(x, g_ref[...]).astype(BF16)
        o_ref[...] = x

    h = h_ref[...]
    gate = _dot(h, wg_ref[...].astype(BF16))
    up = _dot(h, wu_ref[...].astype(BF16))
    a = (_silu(gate) * up).astype(BF16)
    o_ref[...] += 0.5 * _dot(a, wo_ref[...])

    if final:
        @pl.when(j == n_j - 1)
        def _():
            o_ref[...] = _rms(o_ref[...], gf_ref[...])


def _ffn(x, g, w_in, layer, w_out, g_final, *, final):
    m = x.shape[0]
    tm = min(TM_FFN, m)
    n_j = D_FF // TF_FFN
    return pl.pallas_call(
        functools.partial(_ffn_kernel, n_j=n_j, final=final),
        out_shape=jax.ShapeDtypeStruct((m, D_MODEL), F32),
        grid=(m // tm, n_j),
        in_specs=[
            pl.BlockSpec((tm, D_MODEL), lambda i, j: (i, 0)),
            pl.BlockSpec((1, D_MODEL), lambda i, j: (0, 0)),
            pl.BlockSpec((None, D_MODEL, TF_FFN), lambda i, j: (layer, 0, j)),
            pl.BlockSpec((None, D_MODEL, TF_FFN), lambda i, j: (layer, 0, j + n_j)),
            pl.BlockSpec((TF_FFN, D_MODEL), lambda i, j: (j, 0)),
            pl.BlockSpec((1, D_MODEL), lambda i, j: (0, 0)),
        ],
        out_specs=pl.BlockSpec((tm, D_MODEL), lambda i, j: (i, 0)),
        scratch_shapes=[pltpu.VMEM((tm, D_MODEL), BF16)],
        compiler_params=_cparams(("parallel", "arbitrary")),
        name="ffn",
    )(x, g, w_in, w_in, w_out, g_final)


def _ffn_up_kernel(x_ref, g_ref, wg_ref, wu_ref, o_ref, h_ref):
    @pl.when(pl.program_id(1) == 0)
    def _():
        h_ref[...] = _rms(x_ref[...], g_ref[...]).astype(BF16)

    h = h_ref[...]
    o_ref[...] = (_silu(_dot(h, wg_ref[...].astype(BF16))) * _dot(h, wu_ref[...].astype(BF16))).astype(BF16)


def _ffn_down_kernel(a_ref, w_ref, x_ref, o_ref):
    o_ref[...] = x_ref[...] + 0.5 * _dot(a_ref[...], w_ref[...])


def _norm_kernel(x_ref, g_ref, o_ref):
    o_ref[...] = _rms(x_ref[...], g_ref[...])


def _ffn_prompt(x, g, w_in, layer, w_out, g_final, *, final):
    m = x.shape[0]
    tm = TM_DENSE
    n_j = D_FF // TF_FFN
    a = pl.pallas_call(
        _ffn_up_kernel,
        out_shape=jax.ShapeDtypeStruct((m, D_FF), BF16),
        grid=(m // tm, n_j),
        in_specs=[
            pl.BlockSpec((tm, D_MODEL), lambda i, j: (i, 0)),
            pl.BlockSpec((1, D_MODEL), lambda i, j: (0, 0)),
            pl.BlockSpec((None, D_MODEL, TF_FFN), lambda i, j: (layer, 0, j)),
            pl.BlockSpec((None, D_MODEL, TF_FFN), lambda i, j: (layer, 0, j + n_j)),
        ],
        out_specs=pl.BlockSpec((tm, TF_FFN), lambda i, j: (i, j)),
        scratch_shapes=[pltpu.VMEM((tm, D_MODEL), BF16)],
        compiler_params=_cparams(("parallel", "arbitrary")),
        name="ffn_up",
    )(x, g, w_in, w_in)
    y = pl.pallas_call(
        _ffn_down_kernel,
        out_shape=jax.ShapeDtypeStruct((m, D_MODEL), F32),
        grid=(m // tm, D_MODEL // TN_DENSE),
        in_specs=[
            pl.BlockSpec((tm, D_FF), lambda i, j: (i, 0)),
            pl.BlockSpec((D_FF, TN_DENSE), lambda i, j: (0, j)),
            pl.BlockSpec((tm, TN_DENSE), lambda i, j: (i, j)),
        ],
        out_specs=pl.BlockSpec((tm, TN_DENSE), lambda i, j: (i, j)),
        compiler_params=_cparams(("parallel", "arbitrary")),
        name="ffn_down",
    )(a, w_out, x)
    if not final:
        return y
    return pl.pallas_call(
        _norm_kernel,
        out_shape=jax.ShapeDtypeStruct((m, D_MODEL), F32),
        grid=(m // tm,),
        in_specs=[pl.BlockSpec((tm, D_MODEL), lambda i: (i, 0)), pl.BlockSpec((1, D_MODEL), lambda i: (0, 0))],
        out_specs=pl.BlockSpec((tm, D_MODEL), lambda i: (i, 0)),
        compiler_params=_cparams(("parallel",)),
        name="final_norm",
    )(y, g_final)


def _rope_cols(y, cos, sin):
    parts = []
    for hh in range(y.shape[1] // HEAD_DIM):
        yh = y[:, hh * HEAD_DIM:(hh + 1) * HEAD_DIM]
        parts.append(yh * cos + pltpu.roll(yh, HEAD_DIM // 2, 1) * sin)
    return parts[0] if len(parts) == 1 else jnp.concatenate(parts, axis=1)


def _proj_kernel(x_ref, g_ref, w_ref, cos_ref, sin_ref, o_ref, h_ref, *, mode, scale):
    j = pl.program_id(1)

    @pl.when(j == 0)
    def _():
        h_ref[...] = _rms(x_ref[...], g_ref[...]).astype(BF16)

    tm = o_ref.shape[0]
    rc = min(ROW_CHUNK, tm)

    def sweep(epilogue):
        for r0 in range(0, tm, rc):
            rs = slice(r0, r0 + rc)
            o_ref[rs, :] = epilogue(_dot(h_ref[rs, :], w_ref[...]), rs)

    rope = lambda y, rs: _rope_cols(y, cos_ref[rs, :], sin_ref[rs, :])
    if mode == "rope":
        sweep(lambda y, rs: rope(y, rs) * scale)
    elif mode == "rope_even":
        @pl.when(j % 2 == 0)
        def _():
            sweep(rope)

        @pl.when(j % 2 == 1)
        def _():
            sweep(lambda y, rs: y)
    elif mode == "sigmoid":
        sweep(lambda y, rs: jax.nn.sigmoid(y))
    else:
        sweep(lambda y, rs: y)


def _proj(x, g, w, col0, ncols, cos, sin, *, mode, scale=1.0, tn=TN_DENSE):
    m = x.shape[0]
    tm = min(TM_DENSE, m)
    tn = min(tn, ncols)
    jb0 = col0 // tn
    n_pos_tiles = cos.shape[0] // tm
    return pl.pallas_call(
        functools.partial(_proj_kernel, mode=mode, scale=scale),
        out_shape=jax.ShapeDtypeStruct((m, ncols), F32),
        grid=(m // tm, ncols // tn),
        in_specs=[
            pl.BlockSpec((tm, D_MODEL), lambda i, j: (i, 0)),
            pl.BlockSpec((1, D_MODEL), lambda i, j: (0, 0)),
            pl.BlockSpec((D_MODEL, tn), lambda i, j: (0, j + jb0)),
            pl.BlockSpec((tm, HEAD_DIM), lambda i, j: (i % n_pos_tiles, 0)),
            pl.BlockSpec((tm, HEAD_DIM), lambda i, j: (i % n_pos_tiles, 0)),
        ],
        out_specs=pl.BlockSpec((tm, tn), lambda i, j: (i, j)),
        scratch_shapes=[pltpu.VMEM((tm, D_MODEL), BF16)],
        compiler_params=_cparams(("parallel", "arbitrary")),
        name="proj_" + mode,
    )(x, g, w, cos, sin)


def _outproj_kernel(a_ref, w_ref, r_ref, o_ref):
    o_ref[...] = r_ref[...] + _dot(a_ref[...].astype(BF16), w_ref[...])


def _outproj(a, w, res):
    m, k = a.shape
    tm = min(TM_DENSE, m)
    tn = TN_DENSE
    return pl.pallas_call(
        _outproj_kernel,
        out_shape=jax.ShapeDtypeStruct((m, D_MODEL), F32),
        grid=(m // tm, D_MODEL // tn),
        in_specs=[
            pl.BlockSpec((tm, k), lambda i, j: (i, 0)),
            pl.BlockSpec((k, tn), lambda i, j: (0, j)),
            pl.BlockSpec((tm, tn), lambda i, j: (i, j)),
        ],
        out_specs=pl.BlockSpec((tm, tn), lambda i, j: (i, j)),
        compiler_params=_cparams(("parallel", "arbitrary")),
        name="outproj",
    )(a, w, res)


def _diff_lambda(lam, lam_init):
    a = jnp.sum(lam[0:1] * lam[1:2], axis=-1, keepdims=True)
    b = jnp.sum(lam[2:3] * lam[3:4], axis=-1, keepdims=True)
    return jnp.exp(a) - jnp.exp(b) + lam_init


def _subln(o, sub, lam_init):
    return _rms(o, sub) * (1.0 - lam_init)


def _diff_prompt_kernel(q_ref, k_ref, v_ref, lam_ref, sub_ref, o_ref, k1_s, k2_s, v_s, *, lam_init):
    k1_s[...] = k_ref[:, :HEAD_DIM].astype(BF16)
    k2_s[...] = k_ref[:, HEAD_DIM:].astype(BF16)
    v_s[...] = v_ref[...].astype(BF16)
    lmb = _diff_lambda(lam_ref[...], lam_init)
    sub = sub_ref[...]
    tq = TQ_DIFF
    hd2 = 2 * HEAD_DIM
    for qi in range(SEQ // tq):
        q0 = qi * tq
        kl = q0 + tq
        q1 = jnp.concatenate([q_ref[q0:q0 + tq, r * hd2:r * hd2 + HEAD_DIM]
                              for r in range(DIFF_REP)], axis=0).astype(BF16)
        q2 = jnp.concatenate([q_ref[q0:q0 + tq, r * hd2 + HEAD_DIM:(r + 1) * hd2]
                              for r in range(DIFF_REP)], axis=0).astype(BF16)
        rows = DIFF_REP * tq
        pos_q = q0 + (lax.broadcasted_iota(jnp.int32, (rows, kl), 0) & (tq - 1))
        mask = lax.broadcasted_iota(jnp.int32, (rows, kl), 1) <= pos_q
        p1 = _masked_softmax(_dot_nt(q1, k1_s[:kl]), mask)
        p2 = _masked_softmax(_dot_nt(q2, k2_s[:kl]), mask)
        a = (p1 - lmb * p2).astype(BF16)
        o = _subln(_dot(a, v_s[:kl]), sub, lam_init)
        for r in range(DIFF_REP):
            o_ref[q0:q0 + tq, r * hd2:(r + 1) * hd2] = o[r * tq:(r + 1) * tq].astype(BF16)


def _diff_prompt(q, kv, lam, sub, lam_init):
    gw = DIFF_REP * 2 * HEAD_DIM
    return pl.pallas_call(
        functools.partial(_diff_prompt_kernel, lam_init=lam_init),
        out_shape=jax.ShapeDtypeStruct((BATCH * SEQ, D_MODEL), BF16),
        grid=(BATCH, DIFF_KV_HEADS),
        in_specs=[
            pl.BlockSpec((SEQ, gw), lambda b, g: (b, g)),
            pl.BlockSpec((SEQ, 2 * HEAD_DIM), lambda b, g: (b, g)),
            pl.BlockSpec((SEQ, 2 * HEAD_DIM), lambda b, g: (b, DIFF_KV_HEADS + g)),
            pl.BlockSpec((4, HEAD_DIM), lambda b, g: (0, 0)),
            pl.BlockSpec((1, 2 * HEAD_DIM), lambda b, g: (0, 0)),
        ],
        out_specs=pl.BlockSpec((SEQ, gw), lambda b, g: (b, g)),
        scratch_shapes=[pltpu.VMEM((SEQ, HEAD_DIM), BF16), pltpu.VMEM((SEQ, HEAD_DIM), BF16),
                        pltpu.VMEM((SEQ, 2 * HEAD_DIM), BF16)],
        compiler_params=_cparams(("parallel", "parallel")),
        name="diff_prompt",
    )(q, kv, kv, lam, sub)


def _diff_sample_kernel(pt_ref, q_ref, kvn_ref, *rest, lam_init, n_steps):
    del pt_ref
    pages = rest[:DIFF_PAGES_PER_STEP]
    lam_ref, sub_ref, o_ref, m_s, l_s, acc_s = rest[DIFF_PAGES_PER_STEP:]
    step = pl.program_id(1)
    hd2 = 2 * HEAD_DIM
    nq = 2 * DIFF_HEADS
    ncol = PAGE_SIZE * SUBLANES

    @pl.when(step == 0)
    def _():
        m_s[...] = jnp.full(m_s.shape, -jnp.inf, F32)
        l_s[...] = jnp.zeros(l_s.shape, F32)
        acc_s[...] = jnp.zeros(acc_s.shape, F32)

    q = q_ref[...]
    qf = jnp.concatenate([q[:, h * hd2 + m * HEAD_DIM:h * hd2 + (m + 1) * HEAD_DIM]
                          for m in range(2) for h in range(DIFF_HEADS)], axis=0)
    qb = qf.astype(BF16)
    row = lax.broadcasted_iota(jnp.int32, (nq, ncol), 0)
    col = lax.broadcasted_iota(jnp.int32, (nq, ncol), 1)
    own = (((col >> 2) & 1) == (row >> 3)) & ((col & 3) == ((row >> 1) & 3))
    bias = jnp.where(own, 0.0, -jnp.inf)
    s_list = [_dot_nt(qb, pg[:, 0].reshape(ncol, HEAD_DIM).astype(BF16)) + bias for pg in pages]
    m_prev = m_s[:, :1]
    m_new = jnp.maximum(m_prev, jnp.max(functools.reduce(jnp.maximum, s_list), axis=-1, keepdims=True))
    corr = jnp.exp(m_prev - m_new)
    l_new = corr * l_s[:, :1]
    acc = jnp.concatenate([corr, corr], axis=0) * acc_s[...]
    for s, pg in zip(s_list, pages):
        p = jnp.exp(s - m_new)
        l_new = l_new + jnp.sum(p, axis=-1, keepdims=True)
        p1, p2 = p[:DIFF_HEADS], p[DIFF_HEADS:]
        p_lo = jnp.concatenate([p1, _roll_lanes(p2, LANES - DIFF_KV_HEADS)], axis=0)
        p_hi = jnp.concatenate([_roll_lanes(p1, DIFF_KV_HEADS), p2], axis=0)
        lhs = jnp.concatenate([p_lo, p_hi], axis=0).astype(BF16)
        acc = acc + _dot(lhs, pg[:, 1].reshape(ncol, HEAD_DIM).astype(BF16))
    m_s[...] = jnp.broadcast_to(m_new, m_s.shape)
    l_s[...] = jnp.broadcast_to(l_new, l_s.shape)
    acc_s[...] = acc

    @pl.when(step == n_steps - 1)
    def _():
        lmb = _diff_lambda(lam_ref[...], lam_init)
        kvn = kvn_ref[...]
        kv_off = DIFF_KV_HEADS * hd2
        head_rows = lambda f: jnp.concatenate([f(m, h // DIFF_REP) for m in range(2) for h in range(DIFF_HEADS)], 0)
        k_new = head_rows(lambda m, g: kvn[:, g * hd2 + m * HEAD_DIM:g * hd2 + (m + 1) * HEAD_DIM])
        v_lo = head_rows(lambda m, g: kvn[:, kv_off + g * hd2:kv_off + g * hd2 + HEAD_DIM])
        v_hi = head_rows(lambda m, g: kvn[:, kv_off + g * hd2 + HEAD_DIM:kv_off + (g + 1) * hd2])
        s_new = jnp.sum(qf * k_new, axis=-1, keepdims=True)
        m_old = m_s[:, :1]
        m_fin = jnp.maximum(m_old, s_new)
        c_old = jnp.exp(m_old - m_fin)
        p_new = jnp.exp(s_new - m_fin)
        inv = 1.0 / (c_old * l_s[:, :1] + p_new)
        on = jnp.concatenate([(c_old * acc_s[:nq] + p_new * v_lo) * inv,
                              (c_old * acc_s[nq:] + p_new * v_hi) * inv], axis=1)
        o = on[:DIFF_HEADS] - lmb * on[DIFF_HEADS:]
        o_ref[...] = _subln(o, sub_ref[...], lam_init)


def _diff_sample(q, kvn, cache, layer, page_table, lam, sub, lam_init):
    n_steps = N_PAGES // DIFF_PAGES_PER_STEP
    hd2 = 2 * HEAD_DIM
    nq = 2 * DIFF_HEADS
    page_specs = [
        pl.BlockSpec((None, None, PAGE_SIZE, 2, SUBLANES, HEAD_DIM),
                     functools.partial(lambda b, s, pt, k: (layer, pt[b, s * DIFF_PAGES_PER_STEP + k], 0, 0, 0, 0),
                                       k=k))
        for k in range(DIFF_PAGES_PER_STEP)
    ]
    grid_spec = pltpu.PrefetchScalarGridSpec(
        num_scalar_prefetch=1,
        grid=(DEC_BATCH, n_steps),
        in_specs=[
            pl.BlockSpec((None, 1, D_MODEL), lambda b, s, pt: (b, 0, 0)),
            pl.BlockSpec((None, 1, D_MODEL), lambda b, s, pt: (b, 0, 0)),
            *page_specs,
            pl.BlockSpec((4, HEAD_DIM), lambda b, s, pt: (0, 0)),
            pl.BlockSpec((1, hd2), lambda b, s, pt: (0, 0)),
        ],
        out_specs=pl.BlockSpec((None, DIFF_HEADS, hd2), lambda b, s, pt: (b, 0, 0)),
        scratch_shapes=[pltpu.VMEM((nq, LANES), F32), pltpu.VMEM((nq, LANES), F32),
                        pltpu.VMEM((2 * nq, HEAD_DIM), F32)],
    )
    return pl.pallas_call(
        functools.partial(_diff_sample_kernel, lam_init=lam_init, n_steps=n_steps),
        out_shape=jax.ShapeDtypeStruct((DEC_BATCH, DIFF_HEADS, hd2), F32),
        grid_spec=grid_spec,
        compiler_params=_cparams(("parallel", "arbitrary")),
        name="diff_sample",
    )(page_table, q, kvn, *([cache] * DIFF_PAGES_PER_STEP), lam, sub)


def _compress_kernel(*refs, n_src, n_prefetch):
    refs = refs[n_prefetch:]
    srcs = refs[:n_src]
    wcat_ref, pe_ref, b1_ref, w2_ref, o_ref, x_s, carry_s = refs[n_src:]
    step = pl.program_id(1)

    @pl.when(step == 0)
    def _():
        carry_s[...] = jnp.zeros(carry_s.shape, F32)

    planes = 2 * NSA_GROUPS
    nc = CMP_CHUNKS_PER_SLAB // CMP_PARTS
    row0 = lax.broadcasted_iota(jnp.int32, (nc, CMP_HIDDEN), 0) == 0
    rows_per_src = CMP_ROWS_PER_SLAB // n_src
    tiles = [(src, t0) for src in srcs for t0 in range(0, rows_per_src, SUBLANES * CMP_STRIDE)]
    tiles_per_part = len(tiles) // CMP_PARTS
    biases = []
    for n in range(2):
        pb = _dot(pe_ref[n].astype(BF16), wcat_ref[n])
        biases.append(pb[0:1, :CMP_HIDDEN] + pb[1:2, CMP_HIDDEN:] + b1_ref[n])
    for part in range(CMP_PARTS):
        part_tiles = tiles[part * tiles_per_part:(part + 1) * tiles_per_part]
        for tp in range(0, tiles_per_part, 2):
            c0 = tp * SUBLANES
            for r in range(CMP_STRIDE):
                ya, yb = [jnp.swapaxes(src[pl.ds(t0 + r, SUBLANES, stride=CMP_STRIDE)], 0, 1)
                          for src, t0 in part_tiles[tp:tp + 2]]
                for p in range(planes):
                    blk = jnp.concatenate([ya[p], yb[p]], axis=0).astype(BF16)
                    g = p % NSA_GROUPS
                    x_s[part, p // NSA_GROUPS, g * nc + c0:g * nc + c0 + 2 * SUBLANES,
                        r * HEAD_DIM:(r + 1) * HEAD_DIM] = blk
        for n in range(2):
            fs = _dot(x_s[part, n], wcat_ref[n])
            for g in range(NSA_GROUPS):
                first = fs[g * nc:(g + 1) * nc, :CMP_HIDDEN]
                second = fs[g * nc:(g + 1) * nc, CMP_HIDDEN:]
                prev_first = jnp.where(row0, carry_s[n, g][0:1], pltpu.roll(first, 1, 0))
                carry_s[n, g] = jnp.broadcast_to(first[nc - 1:nc], (SUBLANES, CMP_HIDDEN))
                hid = _silu(prev_first + second + biases[n])
                o_ref[n, g, part * nc:(part + 1) * nc, :] = _dot(hid.astype(BF16), w2_ref[n])


def _compress_specs(idx):
    return [
        pl.BlockSpec((2, CMP_STRIDE * HEAD_DIM, 2 * CMP_HIDDEN), idx(lambda: (0, 0, 0))),
        pl.BlockSpec((2, SUBLANES, CMP_STRIDE * HEAD_DIM), idx(lambda: (0, 0, 0))),
        pl.BlockSpec((2, 1, CMP_HIDDEN), idx(lambda: (0, 0, 0))),
        pl.BlockSpec((2, CMP_HIDDEN, HEAD_DIM), idx(lambda: (0, 0, 0))),
    ]


_CMP_SCRATCH = [pltpu.VMEM((CMP_PARTS, 2, NSA_GROUPS * CMP_CHUNKS_PER_SLAB // CMP_PARTS, CMP_STRIDE * HEAD_DIM), BF16),
                pltpu.VMEM((2, NSA_GROUPS, SUBLANES, CMP_HIDDEN), F32)]


def _compress_prompt(rows, cw):
    n_slab = SEQ // CMP_ROWS_PER_SLAB
    planes = 2 * NSA_GROUPS
    return pl.pallas_call(
        functools.partial(_compress_kernel, n_src=1, n_prefetch=0),
        out_shape=jax.ShapeDtypeStruct((BATCH, 2, NSA_GROUPS, n_slab * CMP_CHUNKS_PER_SLAB, HEAD_DIM), F32),
        grid=(BATCH, n_slab),
        in_specs=[pl.BlockSpec((CMP_ROWS_PER_SLAB, planes, HEAD_DIM), lambda b, s: (b * n_slab + s, 0, 0))]
        + _compress_specs(lambda f: (lambda b, s: f())),
        out_specs=pl.BlockSpec((None, 2, NSA_GROUPS, CMP_CHUNKS_PER_SLAB, HEAD_DIM), lambda b, s: (b, 0, 0, s, 0)),
        scratch_shapes=_CMP_SCRATCH,
        compiler_params=_cparams(("parallel", "arbitrary")),
        name="compress_prompt",
    )(rows, *cw)


def _compress_sample(cache, layer, page_table, cw):
    n_slab = PAST_LEN // CMP_ROWS_PER_SLAB
    planes = 2 * NSA_GROUPS
    page_specs = [
        pl.BlockSpec((None, None, PAGE_SIZE, planes, HEAD_DIM),
                     functools.partial(lambda b, s, pt, k: (layer, pt[b, s * CMP_PAGES_PER_SLAB + k], 0, 0, 0), k=k))
        for k in range(CMP_PAGES_PER_SLAB)
    ]
    grid_spec = pltpu.PrefetchScalarGridSpec(
        num_scalar_prefetch=1,
        grid=(DEC_BATCH, n_slab),
        in_specs=page_specs + _compress_specs(lambda f: (lambda b, s, pt: f())),
        out_specs=pl.BlockSpec((None, 2, NSA_GROUPS, CMP_CHUNKS_PER_SLAB, HEAD_DIM),
                               lambda b, s, pt: (b, 0, 0, s, 0)),
        scratch_shapes=_CMP_SCRATCH,
    )
    return pl.pallas_call(
        functools.partial(_compress_kernel, n_src=CMP_PAGES_PER_SLAB, n_prefetch=1),
        out_shape=jax.ShapeDtypeStruct((DEC_BATCH, 2, NSA_GROUPS, n_slab * CMP_CHUNKS_PER_SLAB, HEAD_DIM), F32),
        grid_spec=grid_spec,
        compiler_params=_cparams(("parallel", "arbitrary")),
        name="compress_sample",
    )(page_table, *([cache] * CMP_PAGES_PER_SLAB), *cw)


def _topk_rank(score, blk, n_blocks):
    rank = jnp.zeros(score.shape, F32)
    for sp in range(n_blocks):
        c = score[:, sp:sp + 1]
        before = (c > score) | ((c == score) & (blk > sp))
        rank = rank + jnp.where(before, 1.0, 0.0)
    return rank


def _nsa_prompt_kernel(q_ref, kvc_ref, ks_ref, vs_ref, kw_ref, vw_ref, gt_ref, cover_ref, expand_ref,
                       o_ref, ks_s, vs_s, kw_s, vw_s):
    ks_s[...] = ks_ref[...].astype(BF16)
    vs_s[...] = vs_ref[...].astype(BF16)
    kw_s[...] = kw_ref[...].astype(BF16)
    vw_s[...] = vw_ref[...].astype(BF16)
    kc = kvc_ref[0].astype(BF16)
    vc = kvc_ref[1].astype(BF16)
    n_ent = kc.shape[0]
    tq = TQ_NSA
    rows = NSA_REP * tq
    for qi in range(SEQ // tq):
        q0 = qi * tq
        kl = q0 + tq
        qb = jnp.concatenate([q_ref[q0:q0 + tq, r * HEAD_DIM:(r + 1) * HEAD_DIM]
                              for r in range(NSA_REP)], axis=0).astype(BF16)

        pos_c = q0 + lax.broadcasted_iota(jnp.int32, (tq, n_ent), 0)
        ent = lax.broadcasted_iota(jnp.int32, (tq, n_ent), 1)
        mask_c = (ent >= 1) & (ent * CMP_STRIDE + (CMP_LEN - 1 - CMP_STRIDE) <= pos_c)
        p_c = _biased_softmax(_dot_nt(qb, kc), _head_bias(mask_c))
        o_c = _dot(p_c.astype(BF16), vc)

        psum = functools.reduce(lambda a, b: a + b, [p_c[r * tq:(r + 1) * tq] for r in range(NSA_REP)])
        imp = _dot(psum.astype(BF16), cover_ref[...])
        blk = lax.broadcasted_iota(jnp.int32, (tq, LANES), 1)
        cur = (q0 + lax.broadcasted_iota(jnp.int32, (tq, LANES), 0)) >> SEL_SHIFT
        n_blk = kl // SEL_LEN
        if n_blk <= SEL_TOPK:
            sel = jnp.where(blk <= cur, 1.0, 0.0)
        else:
            forced = (blk == 0) | (blk == cur) | (blk == cur - 1)
            score = jnp.where(forced, SEL_FORCE, jnp.where(blk <= cur, imp, -SEL_FORCE))
            rank = _topk_rank(score, blk, n_blk)
            sel = jnp.where((rank < SEL_TOPK) & (blk <= cur), 1.0, 0.0)
        selk = _dot(sel.astype(BF16), expand_ref[:, :kl])
        pos_s = q0 + lax.broadcasted_iota(jnp.int32, (tq, kl), 0)
        key_s = lax.broadcasted_iota(jnp.int32, (tq, kl), 1)
        mask_s = (selk > 0.5) & (key_s <= pos_s)
        p_s = _biased_softmax(_dot_nt(qb, ks_s[:kl]), _head_bias(mask_s))
        o_s = _dot(p_s.astype(BF16), vs_s[:kl])

        w0 = max(0, q0 - WINDOW)
        wl = kl - w0
        rel = (q0 - w0) + lax.broadcasted_iota(jnp.int32, (tq, wl), 0) - lax.broadcasted_iota(jnp.int32, (tq, wl), 1)
        mask_w = (rel >= 0) & (rel < WINDOW)
        p_w = _biased_softmax(_dot_nt(qb, kw_s[w0:kl]), _head_bias(mask_w))
        o_w = _dot(p_w.astype(BF16), vw_s[w0:kl])

        gt = gt_ref[q0:q0 + tq, :]
        for r in range(NSA_REP):
            sl = slice(r * tq, (r + 1) * tq)
            o = (gt[:, 3 * r:3 * r + 1] * o_c[sl] + gt[:, 3 * r + 1:3 * r + 2] * o_s[sl]
                 + gt[:, 3 * r + 2:3 * r + 3] * o_w[sl])
            o_ref[q0:q0 + tq, r * HEAD_DIM:(r + 1) * HEAD_DIM] = o.astype(BF16)


def _nsa_prompt(q, kvc, sel_rows, win_rows, gates, cover, expand):
    gw = NSA_REP * HEAD_DIM
    n_ent = kvc.shape[3]
    row_spec_k = pl.BlockSpec((SEQ, HEAD_DIM), lambda b, g: (b, g))
    row_spec_v = pl.BlockSpec((SEQ, HEAD_DIM), lambda b, g: (b, NSA_GROUPS + g))
    return pl.pallas_call(
        _nsa_prompt_kernel,
        out_shape=jax.ShapeDtypeStruct((BATCH * SEQ, D_MODEL), BF16),
        grid=(BATCH, NSA_GROUPS),
        in_specs=[
            pl.BlockSpec((SEQ, gw), lambda b, g: (b, g)),
            pl.BlockSpec((None, 2, None, n_ent, HEAD_DIM), lambda b, g: (b, 0, g, 0, 0)),
            row_spec_k, row_spec_v, row_spec_k, row_spec_v,
            pl.BlockSpec((SEQ, LANES), lambda b, g: (b, g)),
            pl.BlockSpec(cover.shape, lambda b, g: (0, 0)),
            pl.BlockSpec(expand.shape, lambda b, g: (0, 0)),
        ],
        out_specs=pl.BlockSpec((SEQ, gw), lambda b, g: (b, g)),
        scratch_shapes=[pltpu.VMEM((SEQ, HEAD_DIM), BF16)] * 4,
        compiler_params=_cparams(("parallel", "parallel")),
        name="nsa_prompt",
    )(q, kvc, sel_rows, sel_rows, win_rows, win_rows, gates, cover, expand)


def _head_rows(q):
    parts = [q[:, r * HEAD_DIM:(r + 1) * HEAD_DIM] for r in range(NSA_REP)]
    parts.append(jnp.zeros((SUBLANES - NSA_REP, HEAD_DIM), F32))
    return jnp.concatenate(parts, axis=0)


def _nsa_sample_cmp_kernel(q_ref, kvc_ref, cover_ref, o_ref, idx_ref):
    n_ent = kvc_ref.shape[2]
    ent = lax.broadcasted_iota(jnp.int32, (SUBLANES, n_ent), 1)
    mask_c = ent >= 1
    psums = []
    for bi in range(J1_BATCH):
        qh = _head_rows(q_ref[bi:bi + 1, :])
        p = _masked_softmax(_dot_nt(qh.astype(BF16), kvc_ref[bi, 0].astype(BF16)), mask_c)
        o = _dot(p.astype(BF16), kvc_ref[bi, 1].astype(BF16))
        for r in range(NSA_REP):
            o_ref[bi:bi + 1, r * HEAD_DIM:(r + 1) * HEAD_DIM] = o[r:r + 1]
        psums.append(functools.reduce(lambda a, b: a + b, [p[r:r + 1] for r in range(NSA_REP)]))
    psum = jnp.concatenate(psums, axis=0)
    imp = _dot(psum.astype(BF16), cover_ref[...])
    blk = lax.broadcasted_iota(jnp.int32, imp.shape, 1)
    cur = N_SEL_PAST
    forced = (blk == 0) | (blk == cur) | (blk == cur - 1)
    score = jnp.where(forced, SEL_FORCE, jnp.where(blk <= cur, imp, -SEL_FORCE))
    rank = _topk_rank(score, blk, cur + 1)
    blk_f = blk.astype(F32)
    lane = lax.broadcasted_iota(jnp.int32, (J1_BATCH, LANES), 1)
    out = jnp.zeros((J1_BATCH, LANES), F32)
    for k in range(SEL_TOPK):
        ik = jnp.sum(jnp.where(rank == float(k), blk_f, 0.0), axis=-1, keepdims=True)
        out = out + jnp.where(lane == k, ik, 0.0)
    idx_ref[...] = out.astype(jnp.int32)


def _nsa_sample_cmp(q, kvc, cover):
    gw = NSA_REP * HEAD_DIM
    n_ent = kvc.shape[3]
    return pl.pallas_call(
        _nsa_sample_cmp_kernel,
        out_shape=(jax.ShapeDtypeStruct((DEC_BATCH, D_MODEL), F32),
                   jax.ShapeDtypeStruct((NSA_GROUPS, DEC_BATCH, LANES), jnp.int32)),
        grid=(NSA_GROUPS, DEC_BATCH // J1_BATCH),
        in_specs=[
            pl.BlockSpec((J1_BATCH, gw), lambda g, c: (c, g)),
            pl.BlockSpec((J1_BATCH, 2, None, n_ent, HEAD_DIM), lambda g, c: (c, 0, g, 0, 0)),
            pl.BlockSpec(cover.shape, lambda g, c: (0, 0)),
        ],
        out_specs=(pl.BlockSpec((J1_BATCH, gw), lambda g, c: (c, g)),
                   pl.BlockSpec((None, J1_BATCH, LANES), lambda g, c: (g, c, 0))),
        compiler_params=_cparams(("parallel", "parallel")),
        name="nsa_sample_cmp",
    )(q, kvc, cover)


def _nsa_sample_kernel(pt_ref, idx_ref, q_ref, oc_ref, gt_ref, ksn_ref, vsn_ref, kwn_ref, vwn_ref,
                       win_ref, *rest):
    del pt_ref, idx_ref
    blocks = rest[:SEL_PAST_SLOTS]
    o_ref = rest[SEL_PAST_SLOTS]
    g = pl.program_id(1)
    planes = 2 * NSA_GROUPS
    qh = _head_rows(q_ref[...])
    qb = qh.astype(BF16)

    def own_bias(n_rows, first_token):
        col = lax.broadcasted_iota(jnp.int32, (SUBLANES, n_rows), 1)
        return jnp.where(((col & (planes - 1)) == g) & (col >= first_token * planes), 0.0, -jnp.inf)

    to_values = lambda e: _roll_lanes(e, NSA_GROUPS).astype(BF16)

    bias_s = own_bias(SEL_LEN * planes, 0)
    kv_list = [blk[...].astype(BF16) for blk in blocks]
    s_list = [_dot_nt(qb, kv) + bias_s for kv in kv_list]
    s_new = jnp.sum(qh * ksn_ref[...], axis=-1, keepdims=True)
    m = jnp.maximum(jnp.max(functools.reduce(jnp.maximum, s_list), axis=-1, keepdims=True), s_new)
    e_new = jnp.exp(s_new - m)
    den = e_new
    acc = e_new * vsn_ref[...]
    for s, kv in zip(s_list, kv_list):
        e = jnp.exp(s - m)
        den = den + jnp.sum(e, axis=-1, keepdims=True)
        acc = acc + _dot(to_values(e), kv)
    o_s = acc * (1.0 / jnp.maximum(den, 1e-30))

    win = win_ref[...].astype(BF16)
    s_w = _dot_nt(qb, win) + own_bias(win.shape[0], 1)
    s_wn = jnp.sum(qh * kwn_ref[...], axis=-1, keepdims=True)
    m_w = jnp.maximum(jnp.max(s_w, axis=-1, keepdims=True), s_wn)
    e_w = jnp.exp(s_w - m_w)
    e_wn = jnp.exp(s_wn - m_w)
    den_w = jnp.sum(e_w, axis=-1, keepdims=True) + e_wn
    o_w = (_dot(to_values(e_w), win) + e_wn * vwn_ref[...]) * (1.0 / jnp.maximum(den_w, 1e-30))

    gt = gt_ref[...]
    for r in range(NSA_REP):
        o = (gt[:, 3 * r:3 * r + 1] * oc_ref[:, r * HEAD_DIM:(r + 1) * HEAD_DIM]
             + gt[:, 3 * r + 1:3 * r + 2] * o_s[r:r + 1] + gt[:, 3 * r + 2:3 * r + 3] * o_w[r:r + 1])
        o_ref[:, r * HEAD_DIM:(r + 1) * HEAD_DIM] = o


def _nsa_sample(q, o_c, gates, sel_new, win_new, win_state, cache_sel, layer, page_table, idx):
    gw = NSA_REP * HEAD_DIM
    planes = 2 * NSA_GROUPS
    sub = PAGE_SIZE // SEL_LEN
    slots = [0, 1] + list(range(3, SEL_TOPK))

    def half_page(b, g, pt, ix, slot):
        blk = jnp.minimum(ix[b, g * SEL_TOPK + slot], N_SEL_PAST - 1)
        return pt[b, blk // sub] * sub + blk % sub

    def block_spec(slot):
        return pl.BlockSpec((None, None, SEL_LEN * planes, HEAD_DIM),
                            lambda b, g, pt, ix: (layer, half_page(b, g, pt, ix, slot), 0, 0))

    row = lambda voff: pl.BlockSpec((None, 1, HEAD_DIM), lambda b, g, pt, ix: (b, 0, voff + g))
    grid_spec = pltpu.PrefetchScalarGridSpec(
        num_scalar_prefetch=2,
        grid=(DEC_BATCH, NSA_GROUPS),
        in_specs=[
            pl.BlockSpec((None, 1, gw), lambda b, g, pt, ix: (b, 0, g)),
            pl.BlockSpec((None, 1, gw), lambda b, g, pt, ix: (b, 0, g)),
            pl.BlockSpec((None, 1, LANES), lambda b, g, pt, ix: (b, 0, g)),
            row(0), row(NSA_GROUPS), row(0), row(NSA_GROUPS),
            pl.BlockSpec((None, None, win_state.shape[2], HEAD_DIM), lambda b, g, pt, ix: (layer, b, 0, 0)),
            *[block_spec(s) for s in slots],
        ],
        out_specs=pl.BlockSpec((None, 1, gw), lambda b, g, pt, ix: (b, 0, g)),
    )
    return pl.pallas_call(
        _nsa_sample_kernel,
        out_shape=jax.ShapeDtypeStruct((DEC_BATCH, 1, D_MODEL), F32),
        grid_spec=grid_spec,
        compiler_params=_cparams(("parallel", "arbitrary")),
        name="nsa_sample",
    )(page_table, idx, q, o_c, gates, sel_new, sel_new, win_new, win_new, win_state,
      *([cache_sel] * SEL_PAST_SLOTS))


def _rope_tables(pos):
    half = HEAD_DIM // 2
    inv = ROPE_THETA ** (-jnp.arange(half, dtype=F32) / half)
    ang = pos.astype(F32)[:, None] * inv[None, :]
    cos, sin = jnp.cos(ang), jnp.sin(ang)
    return jnp.concatenate([cos, cos], axis=-1), jnp.concatenate([-sin, sin], axis=-1)


def _cover_table(n_ent, n_sel, n_lanes):
    start = (jnp.arange(n_ent)[:, None] - 1) * CMP_STRIDE
    bstart = jnp.arange(n_lanes)[None, :] * SEL_LEN
    hit = (start < bstart + SEL_LEN) & (start + CMP_LEN > bstart)
    hit = hit & (jnp.arange(n_ent)[:, None] >= 1) & (jnp.arange(n_lanes)[None, :] < n_sel)
    return hit.astype(BF16)


def _expand_table(n_keys):
    return (jnp.arange(LANES)[:, None] == (jnp.arange(n_keys)[None, :] // SEL_LEN)).astype(BF16)


def kernel(x_prompt, x_sample, cache_diff, cache_nsa_cmp, cache_nsa_sel, state_nsa_win, page_table,
           norm_ffn1, w_ffn1_in, w_ffn1_out, norm_mix, norm_ffn2, w_ffn2_in, w_ffn2_out,
           w_diff_qkv, diff_lambda, diff_subln, w_diff_o,
           w_nsa_in, nsa_cmp_pe, nsa_cmp_w1, nsa_cmp_b1, nsa_cmp_w2, w_nsa_o, norm_final):
    xp = x_prompt.reshape(BATCH * SEQ, D_MODEL)
    xs = x_sample.reshape(DEC_BATCH * DEC_SEQ, D_MODEL)
    n_pool = cache_diff.shape[1]
    wb = state_nsa_win.shape[2]
    gw2 = 2 * NSA_GROUPS * HEAD_DIM
    cos_p, sin_p = _rope_tables(jnp.arange(SEQ))
    cos_s, sin_s = _rope_tables(jnp.full((DEC_BATCH,), PAST_LEN))
    qscale = HEAD_DIM ** -0.5
    row2 = lambda v: v.reshape(1, -1)

    outs = {k: [] for k in ("diff_p", "diff_s", "cmp_p", "cmp_s", "sel_p", "sel_s", "win_p", "win_s")}
    for i in range(DEPTH):
        last = i == DEPTH - 1
        layer = i // 2
        xp = _ffn_prompt(xp, row2(norm_ffn1[i]), w_ffn1_in, i, w_ffn1_out[i].astype(BF16),
                  row2(norm_final), final=False)
        xs = _ffn(xs, row2(norm_ffn1[i]), w_ffn1_in, i, w_ffn1_out[i].astype(BF16),
                  row2(norm_final), final=False)
        gmix = row2(norm_mix[i])
        if i % 2 == 0:
            lam_init = 0.8 - 0.6 * math.exp(-0.3 * i)
            w = w_diff_qkv[layer].astype(BF16)
            qd = 2 * DIFF_HEADS * HEAD_DIM
            kd = 2 * DIFF_KV_HEADS * HEAD_DIM
            lam = diff_lambda[layer]
            sub = row2(diff_subln[layer])
            wo = w_diff_o[layer].astype(BF16)
            res = []
            for x, cos, sin in ((xp, cos_p, sin_p), (xs, cos_s, sin_s)):
                q = _proj(x, gmix, w, 0, qd, cos, sin, mode="rope", scale=qscale, tn=2 * TN_DENSE)
                kv = _proj(x, gmix, w, qd, 2 * kd, cos, sin, mode="rope_even", tn=kd)
                res.append((q, kv))
            (qp, kvp), (qs, kvs) = res
            op = _diff_prompt(qp, kvp, lam, sub, lam_init)
            cache = cache_diff.reshape(cache_diff.shape[0], n_pool, PAGE_SIZE, 2, DIFF_KV_HEADS, 2, HEAD_DIM)
            cache = jnp.transpose(cache, (0, 1, 2, 3, 5, 4, 6)).reshape(
                cache_diff.shape[0], n_pool, PAGE_SIZE, 2, 2 * DIFF_KV_HEADS, HEAD_DIM)
            os_ = _diff_sample(qs.reshape(DEC_BATCH, 1, D_MODEL), kvs.reshape(DEC_BATCH, 1, D_MODEL), cache, layer,
                               page_table, lam, sub, lam_init).reshape(DEC_BATCH, D_MODEL)
            xp = _outproj(op, wo, xp)
            xs = _outproj(os_, wo, xs)
            outs["diff_p"].append(kvp.reshape(BATCH, SEQ, 2, DIFF_KV_HEADS, 2 * HEAD_DIM))
            outs["diff_s"].append(kvs.reshape(DEC_BATCH, DEC_SEQ, 2, DIFF_KV_HEADS, 2 * HEAD_DIM))
        else:
            w_in = w_nsa_in[layer]
            w = w_in.astype(BF16)
            qd = NSA_HEADS * HEAD_DIM
            wg = w_in[:, qd + 3 * gw2:].reshape(D_MODEL, NSA_GROUPS, NSA_REP * 3)
            wg = jnp.pad(wg, ((0, 0), (0, 0), (0, LANES - NSA_REP * 3))).reshape(D_MODEL, NSA_GROUPS * LANES)
            wg = wg.astype(BF16)
            wo = w_nsa_o[layer].astype(BF16)
            w1 = nsa_cmp_w1[layer]
            wcat = jnp.concatenate([w1[:, :CMP_STRIDE].reshape(2, CMP_STRIDE * HEAD_DIM, CMP_HIDDEN),
                                    w1[:, CMP_STRIDE:].reshape(2, CMP_STRIDE * HEAD_DIM, CMP_HIDDEN)],
                                   axis=-1).astype(BF16)
            pe = nsa_cmp_pe[layer].reshape(2, 2, CMP_STRIDE * HEAD_DIM)
            pe = jnp.pad(pe, ((0, 0), (0, SUBLANES - 2), (0, 0)))
            cw = (wcat, pe, nsa_cmp_b1[layer].reshape(2, 1, CMP_HIDDEN), nsa_cmp_w2[layer].astype(BF16))
            res = []
            for x, cos, sin in ((xp, cos_p, sin_p), (xs, cos_s, sin_s)):
                q = _proj(x, gmix, w, 0, qd, cos, sin, mode="rope", scale=qscale, tn=2 * TN_DENSE)
                rows = [_proj(x, gmix, w, qd + c * gw2, gw2, cos, sin, mode="rope_even") for c in range(3)]
                gates = _proj(x, gmix, wg, 0, NSA_GROUPS * LANES, cos, sin, mode="sigmoid")
                res.append((q, rows, gates))
            (qp, rows_p, gates_p), (qs, rows_s, gates_s) = res

            kvc_p = _compress_prompt(rows_p[0].reshape(BATCH * SEQ, 2 * NSA_GROUPS, HEAD_DIM), cw)
            n_ent_p = kvc_p.shape[3]
            op = _nsa_prompt(qp, kvc_p, rows_p[1], rows_p[2], gates_p,
                             _cover_table(n_ent_p, SEQ // SEL_LEN, LANES), _expand_table(SEQ))

            cache_c = cache_nsa_cmp.reshape(cache_nsa_cmp.shape[0], n_pool, PAGE_SIZE, 2 * NSA_GROUPS, HEAD_DIM)
            kvc_s = _compress_sample(cache_c, layer, page_table, cw)
            o_c, idx = _nsa_sample_cmp(qs, kvc_s, _cover_table(kvc_s.shape[3], N_SEL_PAST + 1, 2 * LANES))
            idx = jnp.transpose(idx[:, :, :SEL_TOPK], (1, 0, 2)).reshape(DEC_BATCH, NSA_GROUPS * SEL_TOPK)
            planes = 2 * NSA_GROUPS
            cache_s = cache_nsa_sel.reshape(cache_nsa_sel.shape[0], n_pool * (PAGE_SIZE // SEL_LEN),
                                            SEL_LEN * planes, HEAD_DIM)
            win_state = state_nsa_win.reshape(state_nsa_win.shape[0], DEC_BATCH, wb * planes, HEAD_DIM)
            r3 = lambda a: a.reshape(DEC_BATCH, 1, a.shape[-1])
            os_ = _nsa_sample(r3(qs), r3(o_c), r3(gates_s), r3(rows_s[1]), r3(rows_s[2]), win_state, cache_s,
                              layer, page_table, idx).reshape(DEC_BATCH, D_MODEL)
            xp = _outproj(op, wo, xp)
            xs = _outproj(os_, wo, xs)
            shp = (2, NSA_GROUPS, HEAD_DIM)
            outs["cmp_p"].append(rows_p[0].reshape(BATCH, SEQ, *shp))
            outs["cmp_s"].append(rows_s[0].reshape(DEC_BATCH, DEC_SEQ, *shp))
            outs["sel_p"].append(rows_p[1].reshape(BATCH, SEQ, *shp))
            outs["sel_s"].append(rows_s[1].reshape(DEC_BATCH, DEC_SEQ, *shp))
            wp = rows_p[2].reshape(BATCH, SEQ, *shp)
            outs["win_p"].append(wp[:, SEQ - min(WINDOW, SEQ):])
            kvw = jnp.concatenate([state_nsa_win[layer], rows_s[2].reshape(DEC_BATCH, DEC_SEQ, *shp)], axis=1)
            outs["win_s"].append(kvw[:, DEC_SEQ:])
        xp = _ffn_prompt(xp, row2(norm_ffn2[i]), w_ffn2_in, i, w_ffn2_out[i].astype(BF16),
                  row2(norm_final), final=last)
        xs = _ffn(xs, row2(norm_ffn2[i]), w_ffn2_in, i, w_ffn2_out[i].astype(BF16),
                  row2(norm_final), final=last)
    st = lambda k: jnp.stack(outs[k], axis=0)
    return (xp.reshape(BATCH, SEQ, D_MODEL), xs.reshape(DEC_BATCH, DEC_SEQ, D_MODEL),
            st("diff_p"), st("diff_s"), st("cmp_p"), st("cmp_s"), st("sel_p"), st("sel_s"),
            st("win_p"), st("win_s"))
```
